```python
import math
import jax, jax.numpy as jnp
from jax import lax
import numpy as np

D_MODEL = 2048
BATCH = 16
SEQ = 2048
DEPTH = 4
DEC_BATCH = 16
DEC_SEQ = 16
PAST_LEN = 2048

CHUNK = 64
N_A_LAYERS = DEPTH // 2
N_B_LAYERS = DEPTH - N_A_LAYERS
HEAD_DIM = 128
N_HEADS_A = D_MODEL // HEAD_DIM
N_KV_A = 4
GROUP_A = N_HEADS_A // N_KV_A
N_IDX_HEADS = 16
IDX_DIM = 64
TOPK_MAX = 256
Q_BLOCK = 128
IDX_SCALE = IDX_DIM ** -0.5 * N_IDX_HEADS ** -0.5
A_Q = N_HEADS_A * HEAD_DIM
A_KV = N_KV_A * HEAD_DIM
A_QI = N_IDX_HEADS * IDX_DIM
A_IN_COLS = A_Q + 2 * A_KV + A_QI + IDX_DIM + N_IDX_HEADS
N_HEADS_B = D_MODEL // HEAD_DIM
LEFT_CHUNKS = 8
B_WINDOW = LEFT_CHUNKS * CHUNK
REL_CLIP = 128
N_REL = 2 * REL_CLIP + 1
ATTN_SCALE = HEAD_DIM ** -0.5
N_GROUPS = 4
EXPERTS_PER_GROUP = 8
N_EXPERTS = N_GROUPS * EXPERTS_PER_GROUP
TOP_K_INNER = 2
D_EXPERT = 512
MOE_BLOCK = 128
DN_ALPHA = (2 * DEPTH) ** 0.25
DN_BETA = (8 * DEPTH) ** -0.25
LN_EPS = 1e-5

kernel_name = "yoco_dsa_chunkband_hmoe_step"


def layer_norm(x, g, b):
    xf = x.astype(jnp.float32)
    mu = jnp.mean(xf, axis=-1, keepdims=True)
    var = jnp.mean(jnp.square(xf - mu), axis=-1, keepdims=True)
    return ((xf - mu) * lax.rsqrt(var + LN_EPS) * g + b).astype(x.dtype)


def post_norm(x, y, g, b):
    return layer_norm(DN_ALPHA * x + y, g, b)


def alibi_slopes(n):
    return 2.0 ** (-8.0 * jnp.arange(1, n + 1, dtype=jnp.float32) / n)


def dsa_attend(q, qi, wi, q_pos, k, v, ki, slopes, k_top):
    L = k.shape[1]
    k_pos = jnp.arange(L, dtype=jnp.int32)
    q_chunk = q_pos // CHUNK
    admissible = (k_pos[None, :] // CHUNK) <= q_chunk[:, None]
    s = jnp.einsum('bqhd,bld->bqhl', qi.astype(jnp.float32), ki.astype(jnp.float32))
    score = jnp.einsum('bqhl,bqh->bql', jax.nn.relu(s), wi.astype(jnp.float32)) * IDX_SCALE
    score = jnp.where(admissible[None], score, -jnp.inf)
    _, sel = lax.top_k(score, k_top)
    valid = (sel // CHUNK) <= q_chunk[None, :, None]
    kg = jax.vmap(lambda kb, ib: kb[ib])(k, sel)
    vg = jax.vmap(lambda vb, ib: vb[ib])(v, sel)
    B_, Q_ = q.shape[0], q.shape[1]
    qg = q.reshape(B_, Q_, N_KV_A, GROUP_A, HEAD_DIM)
    logits = jnp.einsum('bqgrd,bqkgd->bqgrk', qg, kg).astype(jnp.float32) * ATTN_SCALE
    dist = jnp.abs(q_pos[None, :, None] - sel).astype(jnp.float32)
    alibi = -slopes.reshape(N_KV_A, GROUP_A)[None, None, :, :, None] * dist[:, :, None, None, :]
    logits = jnp.where(valid[:, :, None, None, :], logits + alibi, -jnp.inf)
    p = jax.nn.softmax(logits, axis=-1).astype(v.dtype)
    o = jnp.einsum('bqgrk,bqkgd->bqgrd', p, vg)
    return o.reshape(B_, Q_, A_Q)


def mixer_a(x, w_in, w_o, slopes, cache):
    B_, T, _ = x.shape
    proj = x @ w_in
    q = proj[..., :A_Q].reshape(B_, T, N_HEADS_A, HEAD_DIM)
    k = proj[..., A_Q:A_Q + A_KV].reshape(B_, T, N_KV_A, HEAD_DIM)
    v = proj[..., A_Q + A_KV:A_Q + 2 * A_KV].reshape(B_, T, N_KV_A, HEAD_DIM)
    o3 = A_Q + 2 * A_KV
    qi = proj[..., o3:o3 + A_QI].reshape(B_, T, N_IDX_HEADS, IDX_DIM)
    ki = proj[..., o3 + A_QI:o3 + A_QI + IDX_DIM]
    wi = proj[..., o3 + A_QI + IDX_DIM:]
    if cache is None:
        past = 0
        kf, vf, kif = k, v, ki
    else:
        ck, cv, ci = cache
        past = ck.shape[1]
        kf = jnp.concatenate([ck, k], axis=1)
        vf = jnp.concatenate([cv, v], axis=1)
        kif = jnp.concatenate([ci, ki], axis=1)
    L = past + T
    k_top = min(TOPK_MAX, L // 4)
    q_pos = past + jnp.arange(T, dtype=jnp.int32)

    def attend(args):
        qb, qib, wib, qpb = args
        return dsa_attend(qb, qib, wib, qpb, kf, vf, kif, slopes, k_top)

    if T > Q_BLOCK:
        nb = T // Q_BLOCK
        blocks = lambda a: jnp.moveaxis(a.reshape((B_, nb, Q_BLOCK) + a.shape[2:]), 1, 0)
        o = lax.map(attend, (blocks(q), blocks(qi), blocks(wi), q_pos.reshape(nb, Q_BLOCK)))
        o = jnp.moveaxis(o, 0, 1).reshape(B_, T, A_Q)
    else:
        o = attend((q, qi, wi, q_pos))
    return o @ w_o, k, v, ki


def band_attend(q, k, v, q_pos, k_pos, k_valid, table):
    qc = q_pos // CHUNK
    kc = k_pos // CHUNK
    mask = k_valid[None, :] & (kc[None, :] <= qc[:, None]) & (kc[None, :] >= qc[:, None] - LEFT_CHUNKS)
    rel = jnp.clip(q_pos[:, None] - k_pos[None, :], -REL_CLIP, REL_CLIP) + REL_CLIP
    bias = table[:, rel].astype(jnp.float32)
    logits = jnp.einsum('bqhd,bkhd->bhqk', q, k).astype(jnp.float32) * ATTN_SCALE + bias[None]
    logits = jnp.where(mask[None, None], logits, -jnp.inf)
    p = jax.nn.softmax(logits, axis=-1).astype(v.dtype)
    o = jnp.einsum('bhqk,bkhd->bqhd', p, v)
    return o.reshape(o.shape[0], o.shape[1], N_HEADS_B * HEAD_DIM)


def mixer_b(x, w_q, w_o, table, k_ctx, v_ctx, ctx_pos0, q_pos0, prompt):
    B_, T, _ = x.shape
    q = (x @ w_q).reshape(B_, T, N_HEADS_B, HEAD_DIM)
    if prompt:
        nc = T // CHUNK
        band = B_WINDOW + CHUNK

        def chunk_fn(c):
            start = c * CHUNK
            qb = lax.dynamic_slice_in_dim(q, start, CHUNK, axis=1)
            kb = lax.dynamic_slice_in_dim(k_ctx, start, band, axis=1)
            vb = lax.dynamic_slice_in_dim(v_ctx, start, band, axis=1)
            qp = start + jnp.arange(CHUNK, dtype=jnp.int32)
            kp = start - B_WINDOW + jnp.arange(band, dtype=jnp.int32)
            return band_attend(qb, kb, vb, qp, kp, kp >= 0, table)

        o = lax.map(chunk_fn, jnp.arange(nc, dtype=jnp.int32))
        o = jnp.moveaxis(o, 0, 1).reshape(B_, T, N_HEADS_B * HEAD_DIM)
    else:
        qp = q_pos0 + jnp.arange(T, dtype=jnp.int32)
        kp = ctx_pos0 + jnp.arange(k_ctx.shape[1], dtype=jnp.int32)
        o = band_attend(q, k_ctx, v_ctx, qp, kp, kp >= 0, table)
    return o @ w_o


def hier_moe(x, w_group, b_group, w_router, b_router, w_gate, w_up, w_down):
    B_, T, D = x.shape
    xt = x.reshape(B_ * T, D)
    N = xt.shape[0]
    g_logits = (xt @ w_group).astype(jnp.float32) + b_group
    g_prob = jax.nn.softmax(g_logits, axis=-1)
    _, g_sel = lax.top_k(g_logits, 1)
    p_group = jnp.take_along_axis(g_prob, g_sel, axis=1)[:, 0]
    e_all = ((xt @ w_router).astype(jnp.float32) + b_router).reshape(N, N_GROUPS, EXPERTS_PER_GROUP)
    e_logits = jnp.take_along_axis(e_all, g_sel[:, :, None], axis=1)[:, 0]
    top_v, top_i = lax.top_k(e_logits, TOP_K_INNER)
    top_w = jax.nn.softmax(top_v, axis=-1) * p_group[:, None]
    expert_id = g_sel * EXPERTS_PER_GROUP + top_i
    A = N * TOP_K_INNER
    a_e = expert_id.reshape(A).astype(jnp.int32)
    a_tok = jnp.repeat(jnp.arange(N, dtype=jnp.int32), TOP_K_INNER)
    a_w = top_w.reshape(A)
    order = jnp.argsort(a_e)
    s_e, s_tok, s_w = a_e[order], a_tok[order], a_w[order]
    counts = jnp.zeros((N_EXPERTS,), jnp.int32).at[a_e].add(1)
    padded = (counts + MOE_BLOCK - 1) // MOE_BLOCK * MOE_BLOCK
    off = jnp.cumsum(counts) - counts
    pend = jnp.cumsum(padded)
    poff = pend - padded
    dest = poff[s_e] + (jnp.arange(A, dtype=jnp.int32) - off[s_e])
    P = (A + N_EXPERTS * (MOE_BLOCK - 1) + MOE_BLOCK - 1) // MOE_BLOCK * MOE_BLOCK
    nb = P // MOE_BLOCK
    buf_tok = jnp.zeros((P,), jnp.int32).at[dest].set(s_tok)
    buf_w = jnp.zeros((P,), x.dtype).at[dest].set(s_w.astype(x.dtype))
    starts = jnp.arange(nb, dtype=jnp.int32) * MOE_BLOCK
    block_e = jnp.minimum(jnp.sum(pend[None, :] <= starts[:, None], axis=1), N_EXPERTS - 1)
    xs = xt[buf_tok].reshape(nb, MOE_BLOCK, D)

    def expert_block(args):
        xb, e = args
        h = jax.nn.silu(xb @ w_gate[e]) * (xb @ w_up[e])
        return h @ w_down[e]

    ys = lax.map(expert_block, (xs, block_e)).reshape(P, D)
    out = jnp.zeros_like(xt).at[buf_tok].add(ys * buf_w[:, None])
    return out.reshape(B_, T, D)


def run_trunk(x, caches, weights):
    (a_w_in, a_w_o, b_w_q, b_w_kv, b_w_o, b_rel_bias, ln_mix_g, ln_mix_b, ln_ffn_g, ln_ffn_b,
     moe_w_group, moe_b_group, moe_w_router, moe_b_router, moe_w_gate, moe_w_up, moe_w_down) = weights
    prompt = caches is None
    B_, T, _ = x.shape
    past = 0 if prompt else caches[0].shape[2]
    slopes = alibi_slopes(N_HEADS_A)
    new_k, new_v, new_i = [], [], []
    k_ctx = v_ctx = None
    ctx_pos0 = 0
    b_state_k = b_state_v = None
    for layer in range(DEPTH):
        if layer < N_A_LAYERS:
            cache_l = None if prompt else (caches[0][layer], caches[1][layer], caches[2][layer])
            y, kn, vn, inn = mixer_a(x, a_w_in[layer], a_w_o[layer], slopes, cache_l)
            new_k.append(kn)
            new_v.append(vn)
            new_i.append(inn)
        else:
            j = layer - N_A_LAYERS
            y = mixer_b(x, b_w_q[j], b_w_o[j], b_rel_bias[j], k_ctx, v_ctx, ctx_pos0, past, prompt)
        x = post_norm(x, y, ln_mix_g[layer], ln_mix_b[layer])
        m = hier_moe(x, moe_w_group[layer], moe_b_group[layer], moe_w_router[layer], moe_b_router[layer],
                     moe_w_gate[layer], moe_w_up[layer], moe_w_down[layer])
        x = post_norm(x, m, ln_ffn_g[layer], ln_ffn_b[layer])
        if layer == N_A_LAYERS - 1:
            kv = (x @ b_w_kv).reshape(B_, T, 2, N_HEADS_B, HEAD_DIM)
            kb, vb = kv[:, :, 0], kv[:, :, 1]
            if prompt:
                pad = ((0, 0), (B_WINDOW, 0), (0, 0), (0, 0))
                k_ctx, v_ctx = jnp.pad(kb, pad), jnp.pad(vb, pad)
                ctx_pos0 = -B_WINDOW
                keep = min(B_WINDOW, T)
                b_state_k, b_state_v = kb[:, T - keep:], vb[:, T - keep:]
            else:
                cbk, cbv = caches[3], caches[4]
                k_ctx = jnp.concatenate([cbk, kb], axis=1)
                v_ctx = jnp.concatenate([cbv, vb], axis=1)
                ctx_pos0 = past - cbk.shape[1]
                b_state_k, b_state_v = kb, vb
    return x, jnp.stack(new_k), jnp.stack(new_v), jnp.stack(new_i), b_state_k, b_state_v


def setup_inputs(seed: int = 0) -> dict:
    key = jax.random.key(seed)
    keys = jax.random.split(key, 40)
    f32 = jnp.float32

    def nrm(i, shape, scale):
        return jax.random.normal(keys[i], shape, f32) * scale

    sD = D_MODEL ** -0.5
    b_cache = min(B_WINDOW, PAST_LEN)
    x_prompt = nrm(0, (BATCH, SEQ, D_MODEL), 1.0)
    x_sample = nrm(1, (DEC_BATCH, DEC_SEQ, D_MODEL), 1.0)
    cache_a_k = nrm(2, (N_A_LAYERS, DEC_BATCH, PAST_LEN, N_KV_A, HEAD_DIM), 1.0)
    cache_a_v = nrm(3, (N_A_LAYERS, DEC_BATCH, PAST_LEN, N_KV_A, HEAD_DIM), DN_BETA)
    cache_a_idx = nrm(4, (N_A_LAYERS, DEC_BATCH, PAST_LEN, IDX_DIM), 1.0)
    cache_b_k = nrm(5, (DEC_BATCH, b_cache, N_HEADS_B, HEAD_DIM), 1.0)
    cache_b_v = nrm(6, (DEC_BATCH, b_cache, N_HEADS_B, HEAD_DIM), DN_BETA)
    a_w_in = jnp.concatenate([
        nrm(7, (N_A_LAYERS, D_MODEL, A_Q), sD),
        nrm(8, (N_A_LAYERS, D_MODEL, A_KV), sD),
        nrm(9, (N_A_LAYERS, D_MODEL, A_KV), sD * DN_BETA),
        nrm(10, (N_A_LAYERS, D_MODEL, A_QI), sD),
        nrm(11, (N_A_LAYERS, D_MODEL, IDX_DIM), sD),
        nrm(12, (N_A_LAYERS, D_MODEL, N_IDX_HEADS), sD)], axis=-1)
    a_w_o = nrm(13, (N_A_LAYERS, A_Q, D_MODEL), A_Q ** -0.5 * DN_BETA)
    b_w_q = nrm(14, (N_B_LAYERS, D_MODEL, N_HEADS_B * HEAD_DIM), sD)
    b_w_kv = jnp.concatenate([
        nrm(15, (D_MODEL, N_HEADS_B * HEAD_DIM), sD),
        nrm(16, (D_MODEL, N_HEADS_B * HEAD_DIM), sD * DN_BETA)], axis=-1)
    b_w_o = nrm(17, (N_B_LAYERS, N_HEADS_B * HEAD_DIM, D_MODEL), (N_HEADS_B * HEAD_DIM) ** -0.5 * DN_BETA)
    b_rel_bias = nrm(18, (N_B_LAYERS, N_HEADS_B, N_REL), 0.2)
    ln_mix_g = 1.0 + nrm(19, (DEPTH, D_MODEL), 0.02)
    ln_mix_b = nrm(20, (DEPTH, D_MODEL), 0.02)
    ln_ffn_g = 1.0 + nrm(21, (DEPTH, D_MODEL), 0.02)
    ln_ffn_b = nrm(22, (DEPTH, D_MODEL), 0.02)
    moe_w_group = nrm(23, (DEPTH, D_MODEL, N_GROUPS), sD)
    moe_b_group = nrm(24, (DEPTH, N_GROUPS), 0.01)
    moe_w_router = nrm(25, (DEPTH, D_MODEL, N_EXPERTS), sD)
    moe_b_router = nrm(26, (DEPTH, N_EXPERTS), 0.01)
    moe_w_gate = nrm(27, (DEPTH, N_EXPERTS, D_MODEL, D_EXPERT), sD)
    moe_w_up = nrm(28, (DEPTH, N_EXPERTS, D_MODEL, D_EXPERT), sD)
    moe_w_down = nrm(29, (DEPTH, N_EXPERTS, D_EXPERT, D_MODEL), D_EXPERT ** -0.5 * DN_BETA)
    return {"x_prompt": x_prompt, "x_sample": x_sample,
            "cache_a_k": cache_a_k, "cache_a_v": cache_a_v, "cache_a_idx": cache_a_idx,
            "cache_b_k": cache_b_k, "cache_b_v": cache_b_v,
            "a_w_in": a_w_in, "a_w_o": a_w_o, "b_w_q": b_w_q, "b_w_kv": b_w_kv, "b_w_o": b_w_o,
            "b_rel_bias": b_rel_bias, "ln_mix_g": ln_mix_g, "ln_mix_b": ln_mix_b,
            "ln_ffn_g": ln_ffn_g, "ln_ffn_b": ln_ffn_b,
            "moe_w_group": moe_w_group, "moe_b_group": moe_b_group,
            "moe_w_router": moe_w_router, "moe_b_router": moe_b_router,
            "moe_w_gate": moe_w_gate, "moe_w_up": moe_w_up, "moe_w_down": moe_w_down}


def reference(x_prompt, x_sample, cache_a_k, cache_a_v, cache_a_idx, cache_b_k, cache_b_v,
              a_w_in, a_w_o, b_w_q, b_w_kv, b_w_o, b_rel_bias, ln_mix_g, ln_mix_b, ln_ffn_g, ln_ffn_b,
              moe_w_group, moe_b_group, moe_w_router, moe_b_router, moe_w_gate, moe_w_up, moe_w_down):
    weights = (a_w_in, a_w_o, b_w_q, b_w_kv, b_w_o, b_rel_bias, ln_mix_g, ln_mix_b, ln_ffn_g, ln_ffn_b,
               moe_w_group, moe_b_group, moe_w_router, moe_b_router, moe_w_gate, moe_w_up, moe_w_down)
    y_prompt, a_k_p, a_v_p, a_i_p, b_k_p, b_v_p = run_trunk(x_prompt, None, weights)
    y_sample, a_k_s, a_v_s, a_i_s, b_k_s, b_v_s = run_trunk(
        x_sample, (cache_a_k, cache_a_v, cache_a_idx, cache_b_k, cache_b_v), weights)
    return (y_prompt, y_sample, a_k_p, a_v_p, a_i_p, b_k_p, b_v_p, a_k_s, a_v_s, a_i_s, b_k_s, b_v_s)
```

```python
import functools
import math

import jax
import jax.numpy as jnp
from jax import lax
from jax.experimental import pallas as pl
from jax.experimental.pallas import tpu as pltpu

CHUNK = 64
TOPK_MAX = 256
LEFT_CHUNKS = 8
LN_EPS = 1e-5

LANES = 128
VMEM_LIMIT_BYTES = 56 * 1024 * 1024
MXU_DTYPE = jnp.bfloat16

F32 = jnp.float32
I32 = jnp.int32
NEG_INF = float("-inf")
INT_MIN = -(2 ** 31)


def _pick_tile(n, cap, mult):
    best = None
    for t in range(mult, min(n, cap) + 1, mult):
        if n % t == 0:
            best = t
    if best is None:
        raise ValueError(f"no tile for n={n} cap={cap} mult={mult}")
    return best


def _params(*sem):
    return pltpu.CompilerParams(dimension_semantics=sem, vmem_limit_bytes=VMEM_LIMIT_BYTES)


def _dot(a, b):
    return jnp.dot(a, b, preferred_element_type=F32)


def _dot_nt(a, b):
    return lax.dot_general(a, b, (((1,), (1,)), ((), ())), preferred_element_type=F32)


def _layer_norm(z, g, b):
    mu = jnp.mean(z, axis=-1, keepdims=True)
    zc = z - mu
    var = jnp.mean(zc * zc, axis=-1, keepdims=True)
    return zc * lax.rsqrt(var + LN_EPS) * g + b


def _mm_kernel(x_ref, w_ref, o_ref):
    o_ref[...] = _dot(x_ref[...].astype(MXU_DTYPE), w_ref[...]).astype(o_ref.dtype)


def _mm(x, w, out_dtype):
    n, k = x.shape
    m = w.shape[1]
    tm = _pick_tile(n, 768, 16)
    tn = _pick_tile(m, 2048, LANES)
    return pl.pallas_call(
        _mm_kernel,
        grid=(m // tn, n // tm),
        in_specs=[pl.BlockSpec((tm, k), lambda j, i: (i, 0)),
                  pl.BlockSpec((k, tn), lambda j, i: (0, j))],
        out_specs=pl.BlockSpec((tm, tn), lambda j, i: (i, j)),
        out_shape=jax.ShapeDtypeStruct((n, m), out_dtype),
        compiler_params=_params("arbitrary", "arbitrary"),
    )(x, w)


def _mm_ln_kernel(a_ref, w_ref, x_ref, g_ref, b_ref, o_ref, ob_ref, *, alpha):
    y = _dot(a_ref[...], w_ref[...])
    out = _layer_norm(alpha * x_ref[...] + y, g_ref[...], b_ref[...])
    o_ref[...] = out
    ob_ref[...] = out.astype(ob_ref.dtype)


def _mm_ln(a, w, x, g, b, alpha):
    n, k = a.shape
    d = w.shape[1]
    tm = _pick_tile(n, 384, 16)
    row = lambda i: (i, 0)
    fixed = lambda i: (0, 0)
    return pl.pallas_call(
        functools.partial(_mm_ln_kernel, alpha=alpha),
        grid=(n // tm,),
        in_specs=[pl.BlockSpec((tm, k), row), pl.BlockSpec((k, d), fixed),
                  pl.BlockSpec((tm, d), row), pl.BlockSpec((1, d), fixed),
                  pl.BlockSpec((1, d), fixed)],
        out_specs=[pl.BlockSpec((tm, d), row), pl.BlockSpec((tm, d), row)],
        out_shape=[jax.ShapeDtypeStruct((n, d), F32), jax.ShapeDtypeStruct((n, d), MXU_DTYPE)],
        compiler_params=_params("arbitrary"),
    )(a, w, x, g.reshape(1, d), b.reshape(1, d))


def _row_total(acc, ones_blk):
    return _dot(acc.astype(MXU_DTYPE), ones_blk)


def _dsa_kernel(q_ref, qi_ref, wq_ref, k_ref, v_ref, kia_ref, kib_ref, prev_ref, o_ref,
                key_ref, msk_ref, *, tq, lp, l_valid, q_pos0, k_top, n_kv, group,
                n_idx_heads, idx_dim, head_dim, slopes, idx_scale, attn_scale):
    del prev_ref
    nslab = lp // LANES
    j = pl.program_id(1)
    q_pos = q_pos0 + j * tq + lax.broadcasted_iota(I32, (tq, lp), 0)
    k_pos = lax.broadcasted_iota(I32, (tq, lp), 1)
    admissible = ((k_pos // CHUNK) <= (q_pos // CHUNK)) & (k_pos < l_valid)

    wq = wq_ref[...]
    score = jnp.zeros((tq, lp), F32)
    for p in range(n_idx_heads // 2):
        qi2 = qi_ref[:, p * 2 * idx_dim:(p + 1) * 2 * idx_dim]
        s_a = _dot_nt(qi2, kia_ref[0])
        s_b = _dot_nt(qi2, kib_ref[0])
        w_a = wq[:, idx_dim + 2 * p:idx_dim + 2 * p + 1]
        w_b = wq[:, idx_dim + 2 * p + 1:idx_dim + 2 * p + 2]
        score = score + w_a * jnp.maximum(s_a, 0.0) + w_b * jnp.maximum(s_b, 0.0)
    score = score * idx_scale + 0.0

    bits = pltpu.bitcast(score, I32)
    key = bits ^ ((bits >> 31) & 0x7FFFFFFF)
    key_ref[...] = jnp.where(admissible, key, INT_MIN)

    ones_blk = jnp.ones((LANES, LANES), MXU_DTYPE)

    def count(pred):
        acc = jnp.zeros((tq, LANES), F32)
        for c in range(nslab):
            acc = acc + jnp.where(pred(key_ref[:, c * LANES:(c + 1) * LANES]), 1.0, 0.0)
        return _row_total(acc, ones_blk)

    kf = float(k_top)
    thr = jnp.where(count(lambda x: x >= 0) >= kf, 0, INT_MIN).astype(I32)

    def bit_step(i, t):
        cand = t | lax.shift_left(jnp.int32(1), jnp.asarray(30 - i, dtype=I32))
        return jnp.where(count(lambda x: x >= cand) >= kf, cand, t)

    thr = lax.fori_loop(0, 31, bit_step, thr)
    thr = jnp.maximum(thr, INT_MIN + 1)

    n_gt = count(lambda x: x > thr)
    n_eq = count(lambda x: x == thr)
    for c in range(nslab):
        sl = slice(c * LANES, (c + 1) * LANES)
        msk_ref[:, sl] = jnp.where(key_ref[:, sl] >= thr, 0.0, NEG_INF)

    has_ties = jnp.max(jnp.where(n_gt + n_eq > kf, 1.0, 0.0)) > 0.0

    @pl.when(has_ties)
    def _():
        need = kf - n_gt
        upper = (lax.broadcasted_iota(I32, (LANES, LANES), 0)
                 < lax.broadcasted_iota(I32, (LANES, LANES), 1)).astype(MXU_DTYPE)
        before = jnp.zeros((tq, LANES), F32)
        for c in range(nslab):
            sl = slice(c * LANES, (c + 1) * LANES)
            ks = key_ref[:, sl]
            eq = jnp.where(ks == thr, 1.0, 0.0)
            rank = before + _dot(eq.astype(MXU_DTYPE), upper)
            keep = (ks > thr) | ((ks == thr) & (rank < need))
            msk_ref[:, sl] = jnp.where(keep, 0.0, NEG_INF)
            before = before + _row_total(eq, ones_blk)

    dist = jnp.abs(q_pos - k_pos).astype(F32)
    for g in range(n_kv):
        cols = slice(g * head_dim, (g + 1) * head_dim)
        kg = k_ref[0, :, cols]
        vg = v_ref[0, :, cols]
        q4 = jnp.concatenate(
            [q_ref[:, (g * group + r) * head_dim:(g * group + r + 1) * head_dim] for r in range(group)],
            axis=0)
        logits = _dot_nt(q4, kg) * attn_scale
        for r in range(group):
            h = g * group + r
            lg = logits[r * tq:(r + 1) * tq] - slopes[h] * dist + msk_ref[...]
            m = jnp.max(lg, axis=-1, keepdims=True)
            p = jnp.exp(lg - m)
            denom = jnp.sum(p, axis=-1, keepdims=True)
            o = _dot(p.astype(MXU_DTYPE), vg) / denom
            o_ref[:, h * head_dim:(h + 1) * head_dim] = o.astype(o_ref.dtype)


def _dsa(q, qi, kiwi, k, v, kia, kib, prev, *, row0, tq, l_valid, q_pos0, k_top, cfg):
    nb, lp, a_kv = k.shape
    n_rows, a_q = q.shape
    n_q = cfg["rows"] // tq
    blk0 = row0 // tq
    qmap = lambda b, j: (blk0 + b * n_q + j, 0)
    kmap = lambda b, j: (b, 0, 0)
    kern = functools.partial(
        _dsa_kernel, tq=tq, lp=lp, l_valid=l_valid, q_pos0=q_pos0, k_top=k_top,
        n_kv=cfg["n_kv"], group=cfg["group"], n_idx_heads=cfg["n_idx_heads"],
        idx_dim=cfg["idx_dim"], head_dim=cfg["head_dim"], slopes=cfg["slopes"],
        idx_scale=cfg["idx_scale"], attn_scale=cfg["attn_scale"])
    return pl.pallas_call(
        kern,
        grid=(nb, n_q),
        in_specs=[pl.BlockSpec((tq, a_q), qmap),
                  pl.BlockSpec((tq, qi.shape[1]), qmap),
                  pl.BlockSpec((tq, LANES), qmap),
                  pl.BlockSpec((1, lp, a_kv), kmap),
                  pl.BlockSpec((1, lp, a_kv), kmap),
                  pl.BlockSpec((1, lp, LANES), kmap),
                  pl.BlockSpec((1, lp, LANES), kmap),
                  pl.BlockSpec(memory_space=pl.ANY)],
        out_specs=pl.BlockSpec((tq, a_q), qmap),
        out_shape=jax.ShapeDtypeStruct((n_rows, a_q), MXU_DTYPE),
        scratch_shapes=[pltpu.VMEM((tq, lp), I32), pltpu.VMEM((tq, lp), F32)],
        input_output_aliases={7: 0},
        compiler_params=_params("arbitrary", "arbitrary"),
    )(q, qi, kiwi, k, v, kia, kib, prev)


def _band_kernel(q_ref, k_ref, v_ref, bias_ref, prev_ref, o_ref, *, tq, win, k_pos0, heads, head_dim,
                 attn_scale):
    del prev_ref
    j = pl.program_id(2)
    start = pl.multiple_of(j * tq, tq)
    k_pos = k_pos0 + j * tq + lax.broadcasted_iota(I32, (tq, win), 1)
    invalid = jnp.where(k_pos >= 0, 0.0, NEG_INF)
    for r in range(heads):
        cols = slice(r * head_dim, (r + 1) * head_dim)
        kw = k_ref[0, pl.ds(start, win), cols]
        vw = v_ref[0, pl.ds(start, win), cols]
        lg = _dot_nt(q_ref[:, cols], kw) * attn_scale + bias_ref[r] + invalid
        m = jnp.max(lg, axis=-1, keepdims=True)
        p = jnp.exp(lg - m)
        denom = jnp.sum(p, axis=-1, keepdims=True)
        o = _dot(p.astype(MXU_DTYPE), vw) / denom
        o_ref[:, cols] = o.astype(o_ref.dtype)


def _band(q, kpad, vpad, bias, prev, *, row0, rows, tq, k_pos0, head_dim, attn_scale):
    nb, lk, d = kpad.shape
    n_rows = q.shape[0]
    n_heads, _, win = bias.shape
    hg = 4 if n_heads % 4 == 0 else 1
    wcol = hg * head_dim
    n_q = rows // tq
    blk0 = row0 // tq
    qmap = lambda b, g, j: (blk0 + b * n_q + j, g)
    kmap = lambda b, g, j: (b, 0, g)
    kern = functools.partial(_band_kernel, tq=tq, win=win, k_pos0=k_pos0, heads=hg,
                             head_dim=head_dim, attn_scale=attn_scale)
    return pl.pallas_call(
        kern,
        grid=(nb, n_heads // hg, n_q),
        in_specs=[pl.BlockSpec((tq, wcol), qmap),
                  pl.BlockSpec((1, lk, wcol), kmap),
                  pl.BlockSpec((1, lk, wcol), kmap),
                  pl.BlockSpec((hg, tq, win), lambda b, g, j: (g, 0, 0)),
                  pl.BlockSpec(memory_space=pl.ANY)],
        out_specs=pl.BlockSpec((tq, wcol), qmap),
        out_shape=jax.ShapeDtypeStruct((n_rows, d), MXU_DTYPE),
        input_output_aliases={4: 0},
        compiler_params=_params("arbitrary", "arbitrary", "arbitrary"),
    )(q, kpad, vpad, bias, prev)


def _band_bias(table, tq, window, rel_clip):
    qi = jnp.arange(tq, dtype=I32)[:, None]
    kj = jnp.arange(window + tq, dtype=I32)[None, :]
    rel = jnp.clip(qi - kj + window, -rel_clip, rel_clip) + rel_clip
    qc = qi // CHUNK
    kc = kj // CHUNK - window // CHUNK
    band = (kc <= qc) & (kc >= qc - LEFT_CHUNKS)
    return jnp.where(band[None], table[:, rel].astype(F32), NEG_INF)


def _router_kernel(x_ref, wh_ref, wl_ref, b_ref, ri_ref, rw_ref, cnt_ref, run_ref, *, tm, n_groups,
                   per_group):
    i = pl.program_id(0)

    @pl.when(i == 0)
    def _():
        run_ref[...] = jnp.zeros_like(run_ref)

    x = x_ref[...]
    xh = x.astype(MXU_DTYPE)
    xl = (x - xh.astype(F32)).astype(MXU_DTYPE)
    logits = _dot(xh, wh_ref[...]) + _dot(xl, wh_ref[...]) + _dot(xh, wl_ref[...]) + b_ref[...]

    lane = lax.broadcasted_iota(I32, (tm, LANES), 1)
    gl = jnp.where(lane < n_groups, logits, NEG_INF)
    gmax = jnp.max(gl, axis=-1, keepdims=True)
    gsel = jnp.min(jnp.where(gl == gmax, lane, LANES), axis=-1, keepdims=True)
    p_group = 1.0 / jnp.sum(jnp.exp(gl - gmax), axis=-1, keepdims=True)

    lo = n_groups + gsel * per_group
    el = jnp.where((lane >= lo) & (lane < lo + per_group), logits, NEG_INF)
    v1 = jnp.max(el, axis=-1, keepdims=True)
    i1 = jnp.min(jnp.where(el == v1, lane, LANES), axis=-1, keepdims=True)
    el2 = jnp.where(lane == i1, NEG_INF, el)
    v2 = jnp.max(el2, axis=-1, keepdims=True)
    i2 = jnp.min(jnp.where(el2 == v2, lane, LANES), axis=-1, keepdims=True)
    t = jnp.exp(v2 - v1)
    w1 = p_group / (1.0 + t)
    w2 = p_group * t / (1.0 + t)
    e1 = i1 - n_groups
    e2 = i2 - n_groups

    oh1 = jnp.where(lane == e1, 1.0, 0.0)
    oh2 = jnp.where(lane == e2, 1.0, 0.0)
    lower = (lax.broadcasted_iota(I32, (tm, tm), 1)
             < lax.broadcasted_iota(I32, (tm, tm), 0)).astype(MXU_DTYPE)
    pre1 = _dot(lower, oh1.astype(MXU_DTYPE))
    pre2 = _dot(lower, oh2.astype(MXU_DTYPE))
    c1 = jnp.sum(oh1, axis=0, keepdims=True)
    c2 = jnp.sum(oh2, axis=0, keepdims=True)
    run = run_ref[...]
    r1 = jnp.sum(oh1 * (pre1 + run), axis=-1, keepdims=True)
    r2 = jnp.sum(oh2 * (pre2 + run + c1), axis=-1, keepdims=True)
    run = run + c1 + c2
    run_ref[...] = run
    cnt_ref[...] = run

    ri = jnp.where(lane == 0, e1, jnp.where(lane == 1, e2, jnp.where(
        lane == 2, r1.astype(I32), jnp.where(lane == 3, r2.astype(I32), 0))))
    ri_ref[...] = ri
    rw_ref[...] = jnp.where(lane == 0, w1, jnp.where(lane == 1, w2, 0.0))


def _router(x, wh, wl, bias, n_groups, per_group):
    n, d = x.shape
    tm = _pick_tile(n, 384, 16)
    row = lambda i: (i, 0)
    fixed = lambda i: (0, 0)
    return pl.pallas_call(
        functools.partial(_router_kernel, tm=tm, n_groups=n_groups, per_group=per_group),
        grid=(n // tm,),
        in_specs=[pl.BlockSpec((tm, d), row), pl.BlockSpec((d, LANES), fixed),
                  pl.BlockSpec((d, LANES), fixed), pl.BlockSpec((1, LANES), fixed)],
        out_specs=[pl.BlockSpec((tm, LANES), row), pl.BlockSpec((tm, LANES), row),
                   pl.BlockSpec((1, LANES), fixed)],
        out_shape=[jax.ShapeDtypeStruct((n, LANES), I32), jax.ShapeDtypeStruct((n, LANES), F32),
                   jax.ShapeDtypeStruct((1, LANES), F32)],
        scratch_shapes=[pltpu.VMEM((1, LANES), F32)],
        compiler_params=_params("arbitrary"),
    )(x, wh, wl, bias)


def _dispatch_kernel(dest_ref, x_ref, init_ref, xs_ref, sem, *, tm):
    del init_ref

    def row_copy(r, s):
        return pltpu.make_async_copy(x_ref.at[pl.ds(r, 1)], xs_ref.at[pl.ds(dest_ref[0, 0, 2 * r + s], 1)], sem)

    def start(r, c):
        row_copy(r, 0).start()
        row_copy(r, 1).start()
        return c

    def wait(r, c):
        row_copy(r, 0).wait()
        row_copy(r, 1).wait()
        return c

    lax.fori_loop(0, tm, start, 0)
    lax.fori_loop(0, tm, wait, 0)


def _dispatch(x, dest, p_rows):
    n, d = x.shape
    tm = _pick_tile(n, 256, 8)
    dest3 = dest.reshape(n // tm, 1, 2 * tm)
    init = jnp.zeros((p_rows, d), x.dtype)
    return pl.pallas_call(
        functools.partial(_dispatch_kernel, tm=tm),
        grid=(n // tm,),
        in_specs=[pl.BlockSpec((1, 1, 2 * tm), lambda i: (i, 0, 0), memory_space=pltpu.SMEM),
                  pl.BlockSpec((tm, d), lambda i: (i, 0)),
                  pl.BlockSpec(memory_space=pl.ANY)],
        out_specs=pl.BlockSpec(memory_space=pl.ANY),
        out_shape=jax.ShapeDtypeStruct((p_rows, d), x.dtype),
        scratch_shapes=[pltpu.SemaphoreType.DMA],
        input_output_aliases={2: 0},
        compiler_params=_params("arbitrary"),
    )(dest3, x, init)


def _expert_kernel(be_ref, nu_ref, xs_ref, wg_ref, wu_ref, wd_ref, ys_ref):
    del be_ref
    used = pl.program_id(0) < nu_ref[0]

    @pl.when(used)
    def _():
        x = xs_ref[...].astype(MXU_DTYPE)
        g = _dot(x, wg_ref[0])
        u = _dot(x, wu_ref[0])
        h = g * (1.0 / (1.0 + jnp.exp(-g))) * u
        ys_ref[...] = _dot(h.astype(MXU_DTYPE), wd_ref[0])

    @pl.when(jnp.logical_not(used))
    def _():
        ys_ref[...] = jnp.zeros_like(ys_ref)


def _experts(xs, block_e, n_used, wg, wu, wd, tb):
    p_rows, d = xs.shape
    f = wg.shape[2]
    nb = p_rows // tb
    rmap = lambda i, be, nu: (jnp.minimum(i, nu[0] - 1), 0)
    wmap = lambda i, be, nu: (be[i], 0, 0)
    grid_spec = pltpu.PrefetchScalarGridSpec(
        num_scalar_prefetch=2,
        grid=(nb,),
        in_specs=[pl.BlockSpec((tb, d), rmap), pl.BlockSpec((1, d, f), wmap),
                  pl.BlockSpec((1, d, f), wmap), pl.BlockSpec((1, f, d), wmap)],
        out_specs=pl.BlockSpec((tb, d), lambda i, be, nu: (i, 0)))
    return pl.pallas_call(
        _expert_kernel,
        grid_spec=grid_spec,
        out_shape=jax.ShapeDtypeStruct((p_rows, d), F32),
        compiler_params=_params("arbitrary"),
    )(block_e, n_used, xs, wg, wu, wd)


def _combine_kernel(dest_ref, x_ref, rw_ref, g_ref, b_ref, ys_ref, o_ref, ob_ref, buf, sem, *, tm, alpha):
    def row_copy(r, s):
        return pltpu.make_async_copy(ys_ref.at[pl.ds(dest_ref[0, 0, 2 * r + s], 1)], buf.at[s, pl.ds(r, 1)], sem)

    def start(r, c):
        row_copy(r, 0).start()
        row_copy(r, 1).start()
        return c

    def wait(r, c):
        row_copy(r, 0).wait()
        row_copy(r, 1).wait()
        return c

    lax.fori_loop(0, tm, start, 0)
    lax.fori_loop(0, tm, wait, 0)
    rw = rw_ref[...]
    m = rw[:, 0:1] * buf[0] + rw[:, 1:2] * buf[1]
    out = _layer_norm(alpha * x_ref[...] + m, g_ref[...], b_ref[...])
    o_ref[...] = out
    ob_ref[...] = out.astype(ob_ref.dtype)


def _combine(x, ys, dest, rw, g, b, alpha):
    n, d = x.shape
    tm = _pick_tile(n, 256, 16)
    dest3 = dest.reshape(n // tm, 1, 2 * tm)
    row = lambda i: (i, 0)
    fixed = lambda i: (0, 0)
    return pl.pallas_call(
        functools.partial(_combine_kernel, tm=tm, alpha=alpha),
        grid=(n // tm,),
        in_specs=[pl.BlockSpec((1, 1, 2 * tm), lambda i: (i, 0, 0), memory_space=pltpu.SMEM),
                  pl.BlockSpec((tm, d), row), pl.BlockSpec((tm, LANES), row),
                  pl.BlockSpec((1, d), fixed), pl.BlockSpec((1, d), fixed),
                  pl.BlockSpec(memory_space=pl.ANY)],
        out_specs=[pl.BlockSpec((tm, d), row), pl.BlockSpec((tm, d), row)],
        out_shape=[jax.ShapeDtypeStruct((n, d), F32), jax.ShapeDtypeStruct((n, d), MXU_DTYPE)],
        scratch_shapes=[pltpu.VMEM((2, tm, d), F32), pltpu.SemaphoreType.DMA],
        compiler_params=_params("arbitrary"),
    )(dest3, x, rw, g.reshape(1, d), b.reshape(1, d), ys)


def _moe(x, w_group, b_group, w_router, b_router, wg, wu, wd, ln_g, ln_b, alpha, tb):
    n, d = x.shape
    n_groups = w_group.shape[1]
    n_experts = w_router.shape[1]
    per_group = n_experts // n_groups
    pad = LANES - n_groups - n_experts
    w_all = jnp.pad(jnp.concatenate([w_group, w_router], axis=1), ((0, 0), (0, pad)))
    b_all = jnp.pad(jnp.concatenate([b_group, b_router]), (0, pad)).reshape(1, LANES)
    wh = w_all.astype(MXU_DTYPE)
    wl = (w_all - wh.astype(F32)).astype(MXU_DTYPE)
    ri, rw, cnt = _router(x, wh, wl, b_all, n_groups, per_group)

    counts = cnt[0, :n_experts].astype(I32)
    padded = (counts + tb - 1) // tb * tb
    pend = jnp.cumsum(padded)
    poff = pend - padded
    expert = ri[:, 0:2]
    dest = (poff[expert] + ri[:, 2:4]).reshape(-1)
    p_rows = (2 * n + n_experts * (tb - 1) + tb - 1) // tb * tb
    starts = jnp.arange(p_rows // tb, dtype=I32) * tb
    block_e = jnp.minimum(jnp.sum(pend[None, :] <= starts[:, None], axis=1), n_experts - 1).astype(I32)
    n_used = (pend[-1:] // tb).astype(I32)

    xs = _dispatch(x, dest, p_rows)
    ys = _experts(xs, block_e, n_used, wg, wu, wd, tb)
    return _combine(x, ys, dest, rw, ln_g, ln_b, alpha)


def kernel(x_prompt, x_sample, cache_a_k, cache_a_v, cache_a_idx, cache_b_k, cache_b_v, a_w_in, a_w_o, b_w_q, b_w_kv, b_w_o, b_rel_bias, ln_mix_g, ln_mix_b, ln_ffn_g, ln_ffn_b, moe_w_group, moe_b_group, moe_w_router, moe_b_router, moe_w_gate, moe_w_up, moe_w_down):
    nb_p, t_p, d = x_prompt.shape
    nb_s, t_s, _ = x_sample.shape
    n_a, _, past, n_kv, head_dim = cache_a_k.shape
    idx_dim = cache_a_idx.shape[-1]
    depth = ln_mix_g.shape[0]
    n_heads = d // head_dim
    a_q = n_heads * head_dim
    a_kv = n_kv * head_dim
    n_idx_heads = (a_w_in.shape[-1] - a_q - 2 * a_kv - idx_dim) // (idx_dim + 1)
    a_qi = n_idx_heads * idx_dim
    window = LEFT_CHUNKS * CHUNK
    rel_clip = (b_rel_bias.shape[-1] - 1) // 2
    alpha = (2 * depth) ** 0.25
    attn_scale = head_dim ** -0.5
    np_rows = nb_p * t_p
    ns_rows = nb_s * t_s
    n = np_rows + ns_rows
    assert 2 * idx_dim == LANES and n_idx_heads % 2 == 0 and n_idx_heads <= LANES - idx_dim
    assert t_p % CHUNK == 0 and past % CHUNK == 0 and t_s <= CHUNK and window % t_s == 0

    cfg = dict(n_kv=n_kv, group=n_heads // n_kv, n_idx_heads=n_idx_heads, idx_dim=idx_dim,
               head_dim=head_dim, attn_scale=attn_scale,
               idx_scale=idx_dim ** -0.5 * n_idx_heads ** -0.5,
               slopes=tuple(2.0 ** (-8.0 * (h + 1) / n_heads) for h in range(n_heads)))

    tq_a = _pick_tile(t_p, 128, CHUNK)
    tq_b = _pick_tile(t_p, 256, CHUNK)
    l_s = past + t_s
    lp_s = -(-l_s // LANES) * LANES
    moe_tb = 256

    x = jnp.concatenate([x_prompt.reshape(np_rows, d), x_sample.reshape(ns_rows, d)], axis=0)
    xb = x
    new_k, new_v, new_i = [], [], []
    kb = vb = None
    for layer in range(depth):
        if layer < n_a:
            w = a_w_in[layer]
            o3 = a_q + 2 * a_kv
            w_kiwi = jnp.pad(w[:, o3 + a_qi:], ((0, 0), (0, LANES - idx_dim - n_idx_heads)))
            q = _mm(xb, w[:, :a_q].astype(MXU_DTYPE), MXU_DTYPE)
            kv = _mm(xb, w[:, a_q:o3].astype(MXU_DTYPE), F32)
            qi = _mm(xb, w[:, o3:o3 + a_qi].astype(MXU_DTYPE), MXU_DTYPE)
            kiwi = _mm(xb, w_kiwi.astype(MXU_DTYPE), F32)
            k_new, v_new, ki_new = kv[:, :a_kv], kv[:, a_kv:], kiwi[:, :idx_dim]
            new_k.append(k_new)
            new_v.append(v_new)
            new_i.append(ki_new)

            def key_side(arr_p, arr_s, cache, width):
                kp = arr_p.reshape(nb_p, t_p, width)
                ks = jnp.concatenate([cache.reshape(nb_s, past, width), arr_s.reshape(nb_s, t_s, width)], axis=1)
                ks = jnp.pad(ks, ((0, 0), (0, lp_s - l_s), (0, 0)))
                return kp, ks

            k_p, k_s = key_side(k_new[:np_rows], k_new[np_rows:], cache_a_k[layer], a_kv)
            v_p, v_s = key_side(v_new[:np_rows], v_new[np_rows:], cache_a_v[layer], a_kv)
            i_p, i_s = key_side(ki_new[:np_rows], ki_new[np_rows:], cache_a_idx[layer], idx_dim)
            zpad = ((0, 0), (0, 0), (0, LANES - idx_dim))
            zpad_front = ((0, 0), (0, 0), (LANES - idx_dim, 0))
            cast = lambda a: a.astype(MXU_DTYPE)
            o = jnp.zeros((n, a_q), MXU_DTYPE)
            o = _dsa(q, qi, kiwi, cast(k_p), cast(v_p), cast(jnp.pad(i_p, zpad)), cast(jnp.pad(i_p, zpad_front)), o,
                     row0=0, tq=tq_a, l_valid=t_p, q_pos0=0, k_top=min(TOPK_MAX, t_p // 4),
                     cfg=dict(cfg, rows=t_p))
            o = _dsa(q, qi, kiwi, cast(k_s), cast(v_s), cast(jnp.pad(i_s, zpad)), cast(jnp.pad(i_s, zpad_front)), o,
                     row0=np_rows, tq=t_s, l_valid=l_s, q_pos0=past, k_top=min(TOPK_MAX, l_s // 4),
                     cfg=dict(cfg, rows=t_s))
            w_o = a_w_o[layer]
        else:
            jb = layer - n_a
            q = _mm(xb, b_w_q[jb].astype(MXU_DTYPE), MXU_DTYPE)
            table = b_rel_bias[jb]
            o = jnp.zeros((n, d), MXU_DTYPE)
            o = _band(q, kb[0], vb[0], _band_bias(table, tq_b, window, rel_clip), o,
                      row0=0, rows=t_p, tq=tq_b, k_pos0=-window, head_dim=head_dim, attn_scale=attn_scale)
            o = _band(q, kb[1], vb[1], _band_bias(table, t_s, window, rel_clip), o,
                      row0=np_rows, rows=t_s, tq=t_s, k_pos0=past - window, head_dim=head_dim,
                      attn_scale=attn_scale)
            w_o = b_w_o[jb]
        x, xb = _mm_ln(o, w_o.astype(MXU_DTYPE), x, ln_mix_g[layer], ln_mix_b[layer], alpha)
        x, xb = _moe(x, moe_w_group[layer], moe_b_group[layer], moe_w_router[layer], moe_b_router[layer],
                     moe_w_gate[layer].astype(MXU_DTYPE), moe_w_up[layer].astype(MXU_DTYPE),
                     moe_w_down[layer].astype(MXU_DTYPE), ln_ffn_g[layer], ln_ffn_b[layer], alpha, moe_tb)
        if layer == n_a - 1:
            kvb = _mm(xb, b_w_kv.astype(MXU_DTYPE), F32)
            kb_new, vb_new = kvb[:, :d], kvb[:, d:]

            def band_ctx(arr, cache):
                ctx_p = jnp.pad(arr[:np_rows].reshape(nb_p, t_p, d), ((0, 0), (window, 0), (0, 0)))
                ctx_s = jnp.concatenate([cache.reshape(nb_s, -1, d), arr[np_rows:].reshape(nb_s, t_s, d)], axis=1)
                return ctx_p.astype(MXU_DTYPE), ctx_s.astype(MXU_DTYPE)

            assert cache_b_k.shape[1] == window
            kb = band_ctx(kb_new, cache_b_k)
            vb = band_ctx(vb_new, cache_b_v)

    keep = min(window, t_p)

    def split(arrs, shape_tail):
        st = jnp.stack(arrs)
        return (st[:, :np_rows].reshape((n_a, nb_p, t_p) + shape_tail),
                st[:, np_rows:].reshape((n_a, nb_s, t_s) + shape_tail))

    a_k_p, a_k_s = split(new_k, (n_kv, head_dim))
    a_v_p, a_v_s = split(new_v, (n_kv, head_dim))
    a_i_p, a_i_s = split(new_i, (idx_dim,))
    heads_b = (n_heads, head_dim)
    b_k_p = kb_new[:np_rows].reshape((nb_p, t_p) + heads_b)[:, t_p - keep:]
    b_v_p = vb_new[:np_rows].reshape((nb_p, t_p) + heads_b)[:, t_p - keep:]
    b_k_s = kb_new[np_rows:].reshape((nb_s, t_s) + heads_b)
    b_v_s = vb_new[np_rows:].reshape((nb_s, t_s) + heads_b)
    return (x[:np_rows].reshape(nb_p, t_p, d), x[np_rows:].reshape(nb_s, t_s, d),
            a_k_p, a_v_p, a_i_p, b_k_p, b_v_p, a_k_s, a_v_s, a_i_s, b_k_s, b_v_s)
```

```python
import functools
import math

import jax
import jax.numpy as jnp
from jax import lax
from jax.experimental import pallas as pl
from jax.experimental.pallas import tpu as pltpu

CHUNK = 64
TOPK_MAX = 256
LEFT_CHUNKS = 8
LN_EPS = 1e-5

LANES = 128
VMEM_LIMIT_BYTES = 56 * 1024 * 1024
MXU_DTYPE = jnp.bfloat16

MM_ROWS = 768
LN_ROWS = 384
DMA_ROWS = 256
DMA_UNROLL = 8
MOE_BLOCK_ROWS = 256
DSA_TILE = 256
SEARCH_ROWS = 128
BAND_TILE = 256
BAND_HEADS = 4

F32 = jnp.float32
I32 = jnp.int32
NEG_INF = float("-inf")
INT_MIN = -(2 ** 31)
LOG2E = math.log2(math.e)


def _pick_tile(n, cap, mult):
    best = None
    for t in range(mult, min(n, cap) + 1, mult):
        if n % t == 0:
            best = t
    if best is None:
        raise ValueError(f"no tile for n={n} cap={cap} mult={mult}")
    return best


def _params(*sem):
    return pltpu.CompilerParams(dimension_semantics=sem, vmem_limit_bytes=VMEM_LIMIT_BYTES)


def _dot(a, b):
    return jnp.dot(a, b, preferred_element_type=F32)


def _dot_nt(a, b):
    return lax.dot_general(a, b, (((1,), (1,)), ((), ())), preferred_element_type=F32)


def _layer_norm(z, g, b):
    mu = jnp.mean(z, axis=-1, keepdims=True)
    zc = z - mu
    var = jnp.mean(zc * zc, axis=-1, keepdims=True)
    return zc * lax.rsqrt(var + LN_EPS) * g + b


def _mm_kernel(x_ref, w_ref, o_ref):
    o_ref[...] = _dot(x_ref[...].astype(MXU_DTYPE), w_ref[...]).astype(o_ref.dtype)


def _mm(x, w, out_dtype):
    n, k = x.shape
    m = w.shape[1]
    tm = _pick_tile(n, MM_ROWS, 16)
    tn = _pick_tile(m, 2048, LANES)
    return pl.pallas_call(
        _mm_kernel,
        grid=(m // tn, n // tm),
        in_specs=[pl.BlockSpec((tm, k), lambda j, i: (i, 0)),
                  pl.BlockSpec((k, tn), lambda j, i: (0, j))],
        out_specs=pl.BlockSpec((tm, tn), lambda j, i: (i, j)),
        out_shape=jax.ShapeDtypeStruct((n, m), out_dtype),
        name="projection",
        compiler_params=_params("arbitrary", "arbitrary"),
    )(x, w)


def _mm_ln_kernel(a_ref, w_ref, x_ref, g_ref, b_ref, o_ref, ob_ref, *, alpha):
    y = _dot(a_ref[...], w_ref[...])
    out = _layer_norm(alpha * x_ref[...] + y, g_ref[...], b_ref[...])
    o_ref[...] = out
    ob_ref[...] = out.astype(ob_ref.dtype)


def _mm_ln(a, w, x, g, b, alpha):
    n, k = a.shape
    d = w.shape[1]
    tm = _pick_tile(n, LN_ROWS, 16)
    row = lambda i: (i, 0)
    fixed = lambda i: (0, 0)
    return pl.pallas_call(
        functools.partial(_mm_ln_kernel, alpha=alpha),
        grid=(n // tm,),
        in_specs=[pl.BlockSpec((tm, k), row), pl.BlockSpec((k, d), fixed),
                  pl.BlockSpec((tm, d), row), pl.BlockSpec((1, d), fixed),
                  pl.BlockSpec((1, d), fixed)],
        out_specs=[pl.BlockSpec((tm, d), row), pl.BlockSpec((tm, d), row)],
        out_shape=[jax.ShapeDtypeStruct((n, d), F32), jax.ShapeDtypeStruct((n, d), MXU_DTYPE)],
        name="out_projection_norm",
        compiler_params=_params("arbitrary"),
    )(a, w, x, g.reshape(1, d), b.reshape(1, d))


def _row_total(acc, ones_blk):
    return _dot(acc.astype(MXU_DTYPE), ones_blk)


def _dsa_kernel(q_ref, qi_ref, wq_ref, k_ref, v_ref, kia_ref, kib_ref, prev_ref, o_ref,
                key_ref, msk_ref, *, tq, lk, l_valid, q_pos0, k_top, n_kv, group,
                n_idx_heads, idx_dim, head_dim, slopes, idx_scale, attn_scale):
    del prev_ref
    nslab = lk // LANES
    q_pos = q_pos0 + lax.broadcasted_iota(I32, (tq, lk), 0)
    k_pos = lax.broadcasted_iota(I32, (tq, lk), 1)
    admissible = ((k_pos // CHUNK) <= (q_pos // CHUNK)) & (k_pos < l_valid)

    wq = wq_ref[...]
    score = jnp.zeros((tq, lk), F32)
    for p in range(n_idx_heads // 2):
        qi2 = qi_ref[:, p * 2 * idx_dim:(p + 1) * 2 * idx_dim]
        s_a = _dot_nt(qi2, kia_ref[0])
        s_b = _dot_nt(qi2, kib_ref[0])
        w_a = wq[:, idx_dim + 2 * p:idx_dim + 2 * p + 1]
        w_b = wq[:, idx_dim + 2 * p + 1:idx_dim + 2 * p + 2]
        score = score + w_a * jnp.maximum(s_a, 0.0) + w_b * jnp.maximum(s_b, 0.0)
    score = score * idx_scale + 0.0

    bits = pltpu.bitcast(score, I32)
    key = bits ^ ((bits >> 31) & 0x7FFFFFFF)
    key_ref[...] = jnp.where(admissible, key, INT_MIN)

    ones_blk = jnp.ones((LANES, LANES), MXU_DTYPE)
    rows = min(tq, SEARCH_ROWS)
    groups = [slice(c * rows, (c + 1) * rows) for c in range(tq // rows)]

    def count(rs, pred):
        acc = jnp.zeros((rows, LANES), F32)
        for c in range(nslab):
            acc = acc + jnp.where(pred(key_ref[rs, c * LANES:(c + 1) * LANES]), 1.0, 0.0)
        return _row_total(acc, ones_blk)

    kf = float(k_top)
    thr0 = tuple(jnp.where(count(rs, lambda x: x >= 0) >= kf, 0, INT_MIN).astype(I32) for rs in groups)

    def bit_step(i, thrs):
        bit = lax.shift_left(jnp.int32(1), jnp.asarray(30 - i, dtype=I32))
        out = []
        for rs, t in zip(groups, thrs):
            cand = t | bit
            out.append(jnp.where(count(rs, lambda x, cand=cand: x >= cand) >= kf, cand, t))
        return tuple(out)

    thrs = lax.fori_loop(0, 31, bit_step, thr0)

    for rs, thr in zip(groups, thrs):
        thr = jnp.maximum(thr, INT_MIN + 1)
        n_gt = count(rs, lambda x, thr=thr: x > thr)
        n_eq = count(rs, lambda x, thr=thr: x == thr)
        for c in range(nslab):
            sl = slice(c * LANES, (c + 1) * LANES)
            msk_ref[rs, sl] = jnp.where(key_ref[rs, sl] >= thr, 0.0, NEG_INF)

        has_ties = jnp.max(jnp.where(n_gt + n_eq > kf, 1.0, 0.0)) > 0.0

        @pl.when(has_ties)
        def _(rs=rs, thr=thr, n_gt=n_gt):
            need = kf - n_gt
            upper = (lax.broadcasted_iota(I32, (LANES, LANES), 0)
                     < lax.broadcasted_iota(I32, (LANES, LANES), 1)).astype(MXU_DTYPE)
            before = jnp.zeros((rows, LANES), F32)
            for c in range(nslab):
                sl = slice(c * LANES, (c + 1) * LANES)
                ks = key_ref[rs, sl]
                eq = jnp.where(ks == thr, 1.0, 0.0)
                rank = before + _dot(eq.astype(MXU_DTYPE), upper)
                keep = (ks > thr) | ((ks == thr) & (rank < need))
                msk_ref[rs, sl] = jnp.where(keep, 0.0, NEG_INF)
                before = before + _row_total(eq, ones_blk)

    dist = jnp.abs(q_pos - k_pos).astype(F32)
    for g in range(n_kv):
        cols = slice(g * head_dim, (g + 1) * head_dim)
        kg = k_ref[0, :, cols]
        vg = v_ref[0, :, cols]
        q4 = jnp.concatenate(
            [q_ref[:, (g * group + r) * head_dim:(g * group + r + 1) * head_dim] for r in range(group)],
            axis=0)
        logits = _dot_nt(q4, kg)
        for r in range(group):
            h = g * group + r
            z = logits[r * tq:(r + 1) * tq] * (attn_scale * LOG2E) - (slopes[h] * LOG2E) * dist + msk_ref[...]
            m = jnp.max(z, axis=-1, keepdims=True)
            p = jnp.exp2(z - m)
            denom = jnp.sum(p, axis=-1, keepdims=True)
            o = _dot(p.astype(MXU_DTYPE), vg) / denom
            o_ref[:, h * head_dim:(h + 1) * head_dim] = o.astype(o_ref.dtype)


def _dsa(q, qi, kiwi, k, v, kia, kib, prev, *, row0, rows, tq, tile, lk, l_valid, q_pos0, k_top, cfg):
    nb, _, a_kv = k.shape
    n_rows, a_q = q.shape
    n_q = rows // tq
    blk0 = row0 // tq
    qmap = lambda b: (blk0 + b * n_q + tile, 0)
    kmap = lambda b: (b, 0, 0)
    kern = functools.partial(
        _dsa_kernel, tq=tq, lk=lk, l_valid=l_valid, q_pos0=q_pos0, k_top=k_top,
        n_kv=cfg["n_kv"], group=cfg["group"], n_idx_heads=cfg["n_idx_heads"],
        idx_dim=cfg["idx_dim"], head_dim=cfg["head_dim"], slopes=cfg["slopes"],
        idx_scale=cfg["idx_scale"], attn_scale=cfg["attn_scale"])
    return pl.pallas_call(
        kern,
        grid=(nb,),
        in_specs=[pl.BlockSpec((tq, a_q), qmap),
                  pl.BlockSpec((tq, qi.shape[1]), qmap),
                  pl.BlockSpec((tq, LANES), qmap),
                  pl.BlockSpec((1, lk, a_kv), kmap),
                  pl.BlockSpec((1, lk, a_kv), kmap),
                  pl.BlockSpec((1, lk, LANES), kmap),
                  pl.BlockSpec((1, lk, LANES), kmap),
                  pl.BlockSpec(memory_space=pl.ANY)],
        out_specs=pl.BlockSpec((tq, a_q), qmap),
        out_shape=jax.ShapeDtypeStruct((n_rows, a_q), MXU_DTYPE),
        scratch_shapes=[pltpu.VMEM((tq, lk), I32), pltpu.VMEM((tq, lk), F32)],
        input_output_aliases={7: 0},
        name="dsa_attention",
        compiler_params=_params("arbitrary"),
    )(q, qi, kiwi, k, v, kia, kib, prev)


def _band_kernel(q_ref, k_ref, v_ref, bias_ref, prev_ref, o_ref, *, tq, win, heads, head_dim, attn_scale):
    del prev_ref
    start = pl.multiple_of(pl.program_id(2) * tq, tq)
    for r in range(heads):
        cols = slice(r * head_dim, (r + 1) * head_dim)
        kw = k_ref[0, pl.ds(start, win), cols]
        vw = v_ref[0, pl.ds(start, win), cols]
        z = _dot_nt(q_ref[:, cols], kw) * (attn_scale * LOG2E) + bias_ref[0, r]
        m = jnp.max(z, axis=-1, keepdims=True)
        p = jnp.exp2(z - m)
        denom = jnp.sum(p, axis=-1, keepdims=True)
        o = _dot(p.astype(MXU_DTYPE), vw) / denom
        o_ref[:, cols] = o.astype(o_ref.dtype)


def _band(q, kpad, vpad, bias, prev, *, row0, rows, tq, head_dim, attn_scale):
    nb, lk, d = kpad.shape
    n_rows = q.shape[0]
    n_var, n_heads, _, win = bias.shape
    hg = BAND_HEADS if n_heads % BAND_HEADS == 0 else 1
    wcol = hg * head_dim
    n_q = rows // tq
    blk0 = row0 // tq
    qmap = lambda b, g, j: (blk0 + b * n_q + j, g)
    kmap = lambda b, g, j: (b, 0, g)
    kern = functools.partial(_band_kernel, tq=tq, win=win, heads=hg, head_dim=head_dim,
                             attn_scale=attn_scale)
    return pl.pallas_call(
        kern,
        grid=(nb, n_heads // hg, n_q),
        in_specs=[pl.BlockSpec((tq, wcol), qmap),
                  pl.BlockSpec((1, lk, wcol), kmap),
                  pl.BlockSpec((1, lk, wcol), kmap),
                  pl.BlockSpec((1, hg, tq, win), lambda b, g, j: (jnp.minimum(j, n_var - 1), g, 0, 0)),
                  pl.BlockSpec(memory_space=pl.ANY)],
        out_specs=pl.BlockSpec((tq, wcol), qmap),
        out_shape=jax.ShapeDtypeStruct((n_rows, d), MXU_DTYPE),
        input_output_aliases={4: 0},
        name="band_attention",
        compiler_params=_params("arbitrary", "arbitrary", "arbitrary"),
    )(q, kpad, vpad, bias, prev)


def _band_bias(table, tq, window, rel_clip, k_pos0, n_tiles):
    n_heads = table.shape[0]
    w = window + tq
    ext = w + tq - 1
    x = jnp.arange(ext, dtype=I32)
    rel = jnp.clip(window + tq - 1 - x, -rel_clip, rel_clip) + rel_clip
    e = jnp.pad(table[:, rel].astype(F32) * LOG2E, ((0, 0), (0, 1)))
    skew = jnp.broadcast_to(e[:, None, :], (n_heads, tq, ext + 1)).reshape(n_heads, tq * (ext + 1))
    skew = skew[:, :tq * ext].reshape(n_heads, tq, ext)[:, :, tq - 1:tq - 1 + w]
    qi = jnp.arange(tq, dtype=I32)[:, None]
    kj = jnp.arange(w, dtype=I32)[None, :]
    qc = qi // CHUNK
    kc = kj // CHUNK - window // CHUNK
    band = (kc <= qc) & (kc >= qc - LEFT_CHUNKS)
    n_inv = min(n_tiles, -(-max(0, -k_pos0) // tq))
    n_var = min(n_inv + 1, n_tiles)
    out = []
    for v in range(n_var):
        ok = band & (k_pos0 + v * tq + kj >= 0) if v < n_inv else band
        out.append(jnp.where(ok[None], skew, NEG_INF))
    return jnp.stack(out)


def _router_kernel(x_ref, wh_ref, wl_ref, b_ref, ri_ref, rw_ref, cnt_ref, run_ref, *, tm, n_groups,
                   per_group):
    i = pl.program_id(0)

    @pl.when(i == 0)
    def _():
        run_ref[...] = jnp.zeros_like(run_ref)

    x = x_ref[...]
    xh = x.astype(MXU_DTYPE)
    xl = (x - xh.astype(F32)).astype(MXU_DTYPE)
    logits = _dot(xh, wh_ref[...]) + _dot(xl, wh_ref[...]) + _dot(xh, wl_ref[...]) + b_ref[...]

    lane = lax.broadcasted_iota(I32, (tm, LANES), 1)
    gl = jnp.where(lane < n_groups, logits, NEG_INF)
    gmax = jnp.max(gl, axis=-1, keepdims=True)
    gsel = jnp.min(jnp.where(gl == gmax, lane, LANES), axis=-1, keepdims=True)
    p_group = 1.0 / jnp.sum(jnp.exp(gl - gmax), axis=-1, keepdims=True)

    lo = n_groups + gsel * per_group
    el = jnp.where((lane >= lo) & (lane < lo + per_group), logits, NEG_INF)
    v1 = jnp.max(el, axis=-1, keepdims=True)
    i1 = jnp.min(jnp.where(el == v1, lane, LANES), axis=-1, keepdims=True)
    el2 = jnp.where(lane == i1, NEG_INF, el)
    v2 = jnp.max(el2, axis=-1, keepdims=True)
    i2 = jnp.min(jnp.where(el2 == v2, lane, LANES), axis=-1, keepdims=True)
    t = jnp.exp(v2 - v1)
    w1 = p_group / (1.0 + t)
    w2 = p_group * t / (1.0 + t)
    e1 = i1 - n_groups
    e2 = i2 - n_groups

    oh1 = jnp.where(lane == e1, 1.0, 0.0)
    oh2 = jnp.where(lane == e2, 1.0, 0.0)
    lower = (lax.broadcasted_iota(I32, (tm, tm), 1)
             < lax.broadcasted_iota(I32, (tm, tm), 0)).astype(MXU_DTYPE)
    pre1 = _dot(lower, oh1.astype(MXU_DTYPE))
    pre2 = _dot(lower, oh2.astype(MXU_DTYPE))
    c1 = jnp.sum(oh1, axis=0, keepdims=True)
    c2 = jnp.sum(oh2, axis=0, keepdims=True)
    run = run_ref[...]
    r1 = jnp.sum(oh1 * (pre1 + run), axis=-1, keepdims=True)
    r2 = jnp.sum(oh2 * (pre2 + run + c1), axis=-1, keepdims=True)
    run = run + c1 + c2
    run_ref[...] = run
    cnt_ref[...] = run

    ri = jnp.where(lane == 0, e1, jnp.where(lane == 1, e2, jnp.where(
        lane == 2, r1.astype(I32), jnp.where(lane == 3, r2.astype(I32), 0))))
    ri_ref[...] = ri
    rw_ref[...] = jnp.where(lane == 0, w1, jnp.where(lane == 1, w2, 0.0))


def _router(x, wh, wl, bias, n_groups, per_group):
    n, d = x.shape
    tm = _pick_tile(n, LN_ROWS, 16)
    row = lambda i: (i, 0)
    fixed = lambda i: (0, 0)
    return pl.pallas_call(
        functools.partial(_router_kernel, tm=tm, n_groups=n_groups, per_group=per_group),
        grid=(n // tm,),
        in_specs=[pl.BlockSpec((tm, d), row), pl.BlockSpec((d, LANES), fixed),
                  pl.BlockSpec((d, LANES), fixed), pl.BlockSpec((1, LANES), fixed)],
        out_specs=[pl.BlockSpec((tm, LANES), row), pl.BlockSpec((tm, LANES), row),
                   pl.BlockSpec((1, LANES), fixed)],
        out_shape=[jax.ShapeDtypeStruct((n, LANES), I32), jax.ShapeDtypeStruct((n, LANES), F32),
                   jax.ShapeDtypeStruct((1, LANES), F32)],
        scratch_shapes=[pltpu.VMEM((1, LANES), F32)],
        name="moe_router",
        compiler_params=_params("arbitrary"),
    )(x, wh, wl, bias)


def _dispatch_kernel(dest_ref, x_ref, init_ref, xs_ref, sem, *, tm):
    del init_ref

    def row_copy(r, s):
        return pltpu.make_async_copy(x_ref.at[pl.ds(r, 1)], xs_ref.at[pl.ds(dest_ref[0, 0, 2 * r + s], 1)], sem)

    def start(r, c):
        row_copy(r, 0).start(priority=0)
        row_copy(r, 1).start(priority=1)
        return c

    def wait(r, c):
        row_copy(r, 0).wait()
        row_copy(r, 1).wait()
        return c

    lax.fori_loop(0, tm, start, 0, unroll=DMA_UNROLL)
    lax.fori_loop(0, tm, wait, 0, unroll=DMA_UNROLL)


def _dispatch(x, dest, p_rows):
    n, d = x.shape
    tm = _pick_tile(n, DMA_ROWS, DMA_UNROLL)
    dest3 = dest.reshape(n // tm, 1, 2 * tm)
    init = jnp.zeros((p_rows, d), x.dtype)
    return pl.pallas_call(
        functools.partial(_dispatch_kernel, tm=tm),
        grid=(n // tm,),
        in_specs=[pl.BlockSpec((1, 1, 2 * tm), lambda i: (i, 0, 0), memory_space=pltpu.SMEM),
                  pl.BlockSpec((tm, d), lambda i: (i, 0)),
                  pl.BlockSpec(memory_space=pl.ANY)],
        out_specs=pl.BlockSpec(memory_space=pl.ANY),
        out_shape=jax.ShapeDtypeStruct((p_rows, d), x.dtype),
        scratch_shapes=[pltpu.SemaphoreType.DMA],
        input_output_aliases={2: 0},
        name="moe_dispatch",
        compiler_params=_params("arbitrary"),
    )(dest3, x, init)


def _expert_kernel(be_ref, nu_ref, xs_ref, wg_ref, wu_ref, wd_ref, ys_ref):
    del be_ref
    used = pl.program_id(0) < nu_ref[0]

    @pl.when(used)
    def _():
        x = xs_ref[...].astype(MXU_DTYPE)
        g = _dot(x, wg_ref[0])
        u = _dot(x, wu_ref[0])
        h = g * (1.0 / (1.0 + jnp.exp(-g))) * u
        ys_ref[...] = _dot(h.astype(MXU_DTYPE), wd_ref[0])

    @pl.when(jnp.logical_not(used))
    def _():
        ys_ref[...] = jnp.zeros_like(ys_ref)


def _experts(xs, block_e, n_used, wg, wu, wd, tb):
    p_rows, d = xs.shape
    f = wg.shape[2]
    nb = p_rows // tb
    rmap = lambda i, be, nu: (jnp.minimum(i, nu[0] - 1), 0)
    wmap = lambda i, be, nu: (be[i], 0, 0)
    grid_spec = pltpu.PrefetchScalarGridSpec(
        num_scalar_prefetch=2,
        grid=(nb,),
        in_specs=[pl.BlockSpec((tb, d), rmap), pl.BlockSpec((1, d, f), wmap),
                  pl.BlockSpec((1, d, f), wmap), pl.BlockSpec((1, f, d), wmap)],
        out_specs=pl.BlockSpec((tb, d), lambda i, be, nu: (i, 0)))
    return pl.pallas_call(
        _expert_kernel,
        grid_spec=grid_spec,
        out_shape=jax.ShapeDtypeStruct((p_rows, d), F32),
        name="moe_experts",
        compiler_params=_params("arbitrary"),
    )(block_e, n_used, xs, wg, wu, wd)


def _combine_kernel(dest_ref, dnext_ref, x_ref, rw_ref, g_ref, b_ref, ys_ref, o_ref, ob_ref, buf, sem, *,
                    tm, alpha):
    i = pl.program_id(0)
    slot = i % 2

    def row_copy(dref, r, s, sl):
        return pltpu.make_async_copy(ys_ref.at[pl.ds(dref[0, 0, 2 * r + s], 1)], buf.at[sl, s, pl.ds(r, 1)],
                                     sem.at[sl])

    def issue(dref, sl):
        def body(r, c):
            row_copy(dref, r, 0, sl).start(priority=0)
            row_copy(dref, r, 1, sl).start(priority=1)
            return c
        lax.fori_loop(0, tm, body, 0, unroll=DMA_UNROLL)

    @pl.when(i == 0)
    def _():
        issue(dest_ref, 0)

    @pl.when(i + 1 < pl.num_programs(0))
    def _():
        issue(dnext_ref, 1 - slot)

    def wait(r, c):
        row_copy(dest_ref, r, 0, slot).wait()
        row_copy(dest_ref, r, 1, slot).wait()
        return c

    lax.fori_loop(0, tm, wait, 0, unroll=DMA_UNROLL)
    rw = rw_ref[...]
    m = rw[:, 0:1] * buf[slot, 0] + rw[:, 1:2] * buf[slot, 1]
    out = _layer_norm(alpha * x_ref[...] + m, g_ref[...], b_ref[...])
    o_ref[...] = out
    ob_ref[...] = out.astype(ob_ref.dtype)


def _combine(x, ys, dest, rw, g, b, alpha):
    n, d = x.shape
    tm = _pick_tile(n, DMA_ROWS, 16)
    n_steps = n // tm
    dest3 = dest.reshape(n_steps, 1, 2 * tm)
    row = lambda i: (i, 0)
    fixed = lambda i: (0, 0)
    return pl.pallas_call(
        functools.partial(_combine_kernel, tm=tm, alpha=alpha),
        grid=(n_steps,),
        in_specs=[pl.BlockSpec((1, 1, 2 * tm), lambda i: (i, 0, 0), memory_space=pltpu.SMEM),
                  pl.BlockSpec((1, 1, 2 * tm), lambda i: (jnp.minimum(i + 1, n_steps - 1), 0, 0),
                               memory_space=pltpu.SMEM),
                  pl.BlockSpec((tm, d), row), pl.BlockSpec((tm, LANES), row),
                  pl.BlockSpec((1, d), fixed), pl.BlockSpec((1, d), fixed),
                  pl.BlockSpec(memory_space=pl.ANY)],
        out_specs=[pl.BlockSpec((tm, d), row), pl.BlockSpec((tm, d), row)],
        out_shape=[jax.ShapeDtypeStruct((n, d), F32), jax.ShapeDtypeStruct((n, d), MXU_DTYPE)],
        scratch_shapes=[pltpu.VMEM((2, 2, tm, d), F32), pltpu.SemaphoreType.DMA((2,))],
        name="moe_combine_norm",
        compiler_params=_params("arbitrary"),
    )(dest3, dest3, x, rw, g.reshape(1, d), b.reshape(1, d), ys)


def _moe(x, w_group, b_group, w_router, b_router, wg, wu, wd, ln_g, ln_b, alpha):
    n, d = x.shape
    tb = MOE_BLOCK_ROWS
    n_groups = w_group.shape[1]
    n_experts = w_router.shape[1]
    per_group = n_experts // n_groups
    pad = LANES - n_groups - n_experts
    w_all = jnp.pad(jnp.concatenate([w_group, w_router], axis=1), ((0, 0), (0, pad)))
    b_all = jnp.pad(jnp.concatenate([b_group, b_router]), (0, pad)).reshape(1, LANES)
    wh = w_all.astype(MXU_DTYPE)
    wl = (w_all - wh.astype(F32)).astype(MXU_DTYPE)
    ri, rw, cnt = _router(x, wh, wl, b_all, n_groups, per_group)

    counts = cnt[0, :n_experts].astype(I32)
    padded = (counts + tb - 1) // tb * tb
    pend = jnp.cumsum(padded)
    poff = pend - padded
    expert = ri[:, 0:2]
    dest = (poff[expert] + ri[:, 2:4]).reshape(-1)
    p_rows = (2 * n + n_experts * (tb - 1) + tb - 1) // tb * tb
    starts = jnp.arange(p_rows // tb, dtype=I32) * tb
    block_e = jnp.minimum(jnp.sum(pend[None, :] <= starts[:, None], axis=1), n_experts - 1).astype(I32)
    n_used = (pend[-1:] // tb).astype(I32)

    xs = _dispatch(x, dest, p_rows)
    ys = _experts(xs, block_e, n_used, wg, wu, wd, tb)
    return _combine(x, ys, dest, rw, ln_g, ln_b, alpha)


def kernel(x_prompt, x_sample, cache_a_k, cache_a_v, cache_a_idx, cache_b_k, cache_b_v, a_w_in, a_w_o, b_w_q, b_w_kv, b_w_o, b_rel_bias, ln_mix_g, ln_mix_b, ln_ffn_g, ln_ffn_b, moe_w_group, moe_b_group, moe_w_router, moe_b_router, moe_w_gate, moe_w_up, moe_w_down):
    nb_p, t_p, d = x_prompt.shape
    nb_s, t_s, _ = x_sample.shape
    n_a, _, past, n_kv, head_dim = cache_a_k.shape
    idx_dim = cache_a_idx.shape[-1]
    depth = ln_mix_g.shape[0]
    n_heads = d // head_dim
    a_q = n_heads * head_dim
    a_kv = n_kv * head_dim
    n_idx_heads = (a_w_in.shape[-1] - a_q - 2 * a_kv - idx_dim) // (idx_dim + 1)
    a_qi = n_idx_heads * idx_dim
    window = LEFT_CHUNKS * CHUNK
    rel_clip = (b_rel_bias.shape[-1] - 1) // 2
    alpha = (2 * depth) ** 0.25
    attn_scale = head_dim ** -0.5
    np_rows = nb_p * t_p
    ns_rows = nb_s * t_s
    n = np_rows + ns_rows
    assert 2 * idx_dim == LANES and n_idx_heads % 2 == 0 and n_idx_heads <= LANES - idx_dim
    assert t_p % CHUNK == 0 and past % CHUNK == 0 and t_s <= CHUNK and window % t_s == 0
    assert cache_b_k.shape[1] == window

    cfg = dict(n_kv=n_kv, group=n_heads // n_kv, n_idx_heads=n_idx_heads, idx_dim=idx_dim,
               head_dim=head_dim, attn_scale=attn_scale,
               idx_scale=idx_dim ** -0.5 * n_idx_heads ** -0.5,
               slopes=tuple(2.0 ** (-8.0 * (h + 1) / n_heads) for h in range(n_heads)))

    tq_a = _pick_tile(t_p, DSA_TILE, LANES)
    tq_b = _pick_tile(t_p, BAND_TILE, CHUNK)
    l_s = past + t_s
    lp_s = -(-l_s // LANES) * LANES

    x = jnp.concatenate([x_prompt.reshape(np_rows, d), x_sample.reshape(ns_rows, d)], axis=0)
    xb = x
    new_k, new_v, new_i = [], [], []
    kb = vb = None
    for layer in range(depth):
        if layer < n_a:
            w = a_w_in[layer]
            o3 = a_q + 2 * a_kv
            w_kiwi = jnp.pad(w[:, o3 + a_qi:], ((0, 0), (0, LANES - idx_dim - n_idx_heads)))
            q = _mm(xb, w[:, :a_q].astype(MXU_DTYPE), MXU_DTYPE)
            kv = _mm(xb, w[:, a_q:o3].astype(MXU_DTYPE), F32)
            qi = _mm(xb, w[:, o3:o3 + a_qi].astype(MXU_DTYPE), MXU_DTYPE)
            kiwi = _mm(xb, w_kiwi.astype(MXU_DTYPE), F32)
            k_new, v_new, ki_new = kv[:, :a_kv], kv[:, a_kv:], kiwi[:, :idx_dim]
            new_k.append(k_new)
            new_v.append(v_new)
            new_i.append(ki_new)

            def key_side(arr_p, arr_s, cache, width):
                kp = arr_p.reshape(nb_p, t_p, width)
                ks = jnp.concatenate([cache.reshape(nb_s, past, width), arr_s.reshape(nb_s, t_s, width)], axis=1)
                ks = jnp.pad(ks, ((0, 0), (0, lp_s - l_s), (0, 0)))
                return kp, ks

            k_p, k_s = key_side(k_new[:np_rows], k_new[np_rows:], cache_a_k[layer], a_kv)
            v_p, v_s = key_side(v_new[:np_rows], v_new[np_rows:], cache_a_v[layer], a_kv)
            i_p, i_s = key_side(ki_new[:np_rows], ki_new[np_rows:], cache_a_idx[layer], idx_dim)
            zpad = ((0, 0), (0, 0), (0, LANES - idx_dim))
            zpad_front = ((0, 0), (0, 0), (LANES - idx_dim, 0))
            cast = lambda a: a.astype(MXU_DTYPE)
            keys_p = (cast(k_p), cast(v_p), cast(jnp.pad(i_p, zpad)), cast(jnp.pad(i_p, zpad_front)))
            keys_s = (cast(k_s), cast(v_s), cast(jnp.pad(i_s, zpad)), cast(jnp.pad(i_s, zpad_front)))
            o = jnp.zeros((n, a_q), MXU_DTYPE)
            for u in range(t_p // tq_a):
                o = _dsa(q, qi, kiwi, *keys_p, o, row0=0, rows=t_p, tq=tq_a, tile=u, lk=(u + 1) * tq_a,
                         l_valid=t_p, q_pos0=u * tq_a, k_top=min(TOPK_MAX, t_p // 4), cfg=cfg)
            o = _dsa(q, qi, kiwi, *keys_s, o, row0=np_rows, rows=t_s, tq=t_s, tile=0, lk=lp_s,
                     l_valid=l_s, q_pos0=past, k_top=min(TOPK_MAX, l_s // 4), cfg=cfg)
            w_o = a_w_o[layer]
        else:
            jb = layer - n_a
            q = _mm(xb, b_w_q[jb].astype(MXU_DTYPE), MXU_DTYPE)
            table = b_rel_bias[jb]
            o = jnp.zeros((n, d), MXU_DTYPE)
            o = _band(q, kb[0], vb[0], _band_bias(table, tq_b, window, rel_clip, -window, t_p // tq_b), o,
                      row0=0, rows=t_p, tq=tq_b, head_dim=head_dim, attn_scale=attn_scale)
            o = _band(q, kb[1], vb[1], _band_bias(table, t_s, window, rel_clip, past - window, 1), o,
                      row0=np_rows, rows=t_s, tq=t_s, head_dim=head_dim, attn_scale=attn_scale)
            w_o = b_w_o[jb]
        x, xb = _mm_ln(o, w_o.astype(MXU_DTYPE), x, ln_mix_g[layer], ln_mix_b[layer], alpha)
        x, xb = _moe(x, moe_w_group[layer], moe_b_group[layer], moe_w_router[layer], moe_b_router[layer],
                     moe_w_gate[layer].astype(MXU_DTYPE), moe_w_up[layer].astype(MXU_DTYPE),
                     moe_w_down[layer].astype(MXU_DTYPE), ln_ffn_g[layer], ln_ffn_b[layer], alpha)
        if layer == n_a - 1:
            kvb = _mm(xb, b_w_kv.astype(MXU_DTYPE), F32)
            kb_new, vb_new = kvb[:, :d], kvb[:, d:]

            def band_ctx(arr, cache):
                ctx_p = jnp.pad(arr[:np_rows].reshape(nb_p, t_p, d), ((0, 0), (window, 0), (0, 0)))
                ctx_s = jnp.concatenate([cache.reshape(nb_s, -1, d), arr[np_rows:].reshape(nb_s, t_s, d)], axis=1)
                return ctx_p.astype(MXU_DTYPE), ctx_s.astype(MXU_DTYPE)

            kb = band_ctx(kb_new, cache_b_k)
            vb = band_ctx(vb_new, cache_b_v)

    keep = min(window, t_p)

    def split(arrs, shape_tail):
        st = jnp.stack(arrs)
        return (st[:, :np_rows].reshape((n_a, nb_p, t_p) + shape_tail),
                st[:, np_rows:].reshape((n_a, nb_s, t_s) + shape_tail))

    a_k_p, a_k_s = split(new_k, (n_kv, head_dim))
    a_v_p, a_v_s = split(new_v, (n_kv, head_dim))
    a_i_p, a_i_s = split(new_i, (idx_dim,))
    heads_b = (n_heads, head_dim)
    b_k_p = kb_new[:np_rows].reshape((nb_p, t_p) + heads_b)[:, t_p - keep:]
    b_v_p = vb_new[:np_rows].reshape((nb_p, t_p) + heads_b)[:, t_p - keep:]
    b_k_s = kb_new[np_rows:].reshape((nb_s, t_s) + heads_b)
    b_v_s = vb_new[np_rows:].reshape((nb_s, t_s) + heads_b)
    return (x[:np_rows].reshape(nb_p, t_p, d), x[np_rows:].reshape(nb_s, t_s, d),
            a_k_p, a_v_p, a_i_p, b_k_p, b_v_p, a_k_s, a_v_s, a_i_s, b_k_s, b_v_s)
```

```python
import functools
import math

import jax
import jax.numpy as jnp
from jax import lax
from jax.experimental import pallas as pl
from jax.experimental.pallas import tpu as pltpu

CHUNK = 64
TOPK_MAX = 256
LEFT_CHUNKS = 8
LN_EPS = 1e-5

LANES = 128
VMEM_LIMIT_BYTES = 56 * 1024 * 1024
MXU_DTYPE = jnp.bfloat16

MM_ROWS = 768
LN_ROWS = 384
DMA_ROWS = 256
DMA_UNROLL = 8
MOE_BLOCK_ROWS = 256
ROUTE_FIELDS = 8
DSA_TILE = 256
SEARCH_ROWS = 128
BAND_TILE = 256
BAND_HEADS = 8

F32 = jnp.float32
I32 = jnp.int32
NEG_INF = float("-inf")
INT_MIN = -(2 ** 31)
LOG2E = math.log2(math.e)


def _pick_tile(n, cap, mult):
    best = None
    for t in range(mult, min(n, cap) + 1, mult):
        if n % t == 0:
            best = t
    if best is None:
        raise ValueError(f"no tile for n={n} cap={cap} mult={mult}")
    return best


def _params(*sem):
    return pltpu.CompilerParams(dimension_semantics=sem, vmem_limit_bytes=VMEM_LIMIT_BYTES)


def _dot(a, b):
    return jnp.dot(a, b, preferred_element_type=F32)


def _dot_nt(a, b):
    return lax.dot_general(a, b, (((1,), (1,)), ((), ())), preferred_element_type=F32)


def _layer_norm(z, g, b):
    mu = jnp.mean(z, axis=-1, keepdims=True)
    zc = z - mu
    var = jnp.mean(zc * zc, axis=-1, keepdims=True)
    return zc * lax.rsqrt(var + LN_EPS) * g + b


def _mm_kernel(x_ref, w_ref, o_ref):
    o_ref[...] = _dot(x_ref[...].astype(MXU_DTYPE), w_ref[...]).astype(o_ref.dtype)


def _mm(x, w, out_dtype):
    n, k = x.shape
    m = w.shape[1]
    tm = _pick_tile(n, MM_ROWS, 16)
    tn = _pick_tile(m, 2048, LANES)
    return pl.pallas_call(
        _mm_kernel,
        grid=(m // tn, n // tm),
        in_specs=[pl.BlockSpec((tm, k), lambda j, i: (i, 0)),
                  pl.BlockSpec((k, tn), lambda j, i: (0, j))],
        out_specs=pl.BlockSpec((tm, tn), lambda j, i: (i, j)),
        out_shape=jax.ShapeDtypeStruct((n, m), out_dtype),
        name="projection",
        compiler_params=_params("arbitrary", "arbitrary"),
    )(x, w)


def _mm_dual_kernel(x_ref, w_ref, o_ref, ob_ref):
    y = _dot(x_ref[...].astype(MXU_DTYPE), w_ref[...])
    o_ref[...] = y
    ob_ref[...] = y.astype(ob_ref.dtype)


def _mm_dual(x, w):
    n, k = x.shape
    m = w.shape[1]
    tm = _pick_tile(n, MM_ROWS, 16)
    return pl.pallas_call(
        _mm_dual_kernel,
        grid=(n // tm,),
        in_specs=[pl.BlockSpec((tm, k), lambda i: (i, 0)),
                  pl.BlockSpec((k, m), lambda i: (0, 0))],
        out_specs=[pl.BlockSpec((tm, m), lambda i: (i, 0)), pl.BlockSpec((tm, m), lambda i: (i, 0))],
        out_shape=[jax.ShapeDtypeStruct((n, m), F32), jax.ShapeDtypeStruct((n, m), MXU_DTYPE)],
        name="projection_dual",
        compiler_params=_params("arbitrary"),
    )(x, w)


def _mm_ln_kernel(a_ref, w_ref, x_ref, g_ref, b_ref, o_ref, ob_ref, *, alpha):
    y = _dot(a_ref[...], w_ref[...])
    out = _layer_norm(alpha * x_ref[...] + y, g_ref[...], b_ref[...])
    o_ref[...] = out
    ob_ref[...] = out.astype(ob_ref.dtype)


def _mm_ln(a, w, x, g, b, alpha):
    n, k = a.shape
    d = w.shape[1]
    tm = _pick_tile(n, LN_ROWS, 16)
    row = lambda i: (i, 0)
    fixed = lambda i: (0, 0)
    return pl.pallas_call(
        functools.partial(_mm_ln_kernel, alpha=alpha),
        grid=(n // tm,),
        in_specs=[pl.BlockSpec((tm, k), row), pl.BlockSpec((k, d), fixed),
                  pl.BlockSpec((tm, d), row), pl.BlockSpec((1, d), fixed),
                  pl.BlockSpec((1, d), fixed)],
        out_specs=[pl.BlockSpec((tm, d), row), pl.BlockSpec((tm, d), row)],
        out_shape=[jax.ShapeDtypeStruct((n, d), F32), jax.ShapeDtypeStruct((n, d), MXU_DTYPE)],
        name="out_projection_norm",
        compiler_params=_params("arbitrary"),
    )(a, w, x, g.reshape(1, d), b.reshape(1, d))


def _row_total(acc, ones_blk):
    return _dot(acc.astype(MXU_DTYPE), ones_blk)


def _dsa_kernel(q_ref, qi_ref, wq_ref, k_ref, v_ref, kia_ref, kib_ref, prev_ref, o_ref,
                key_ref, msk_ref, *, tq, lk, l_valid, q_pos0, k_top, n_kv, group,
                n_idx_heads, idx_dim, head_dim, slopes, idx_scale, attn_scale):
    del prev_ref
    nslab = lk // LANES
    q_pos = q_pos0 + lax.broadcasted_iota(I32, (tq, lk), 0)
    k_pos = lax.broadcasted_iota(I32, (tq, lk), 1)
    admissible = ((k_pos // CHUNK) <= (q_pos // CHUNK)) & (k_pos < l_valid)

    wq = wq_ref[...]
    score = jnp.zeros((tq, lk), F32)
    for p in range(n_idx_heads // 2):
        qi2 = qi_ref[:, p * 2 * idx_dim:(p + 1) * 2 * idx_dim]
        s_a = _dot_nt(qi2, kia_ref[0])
        s_b = _dot_nt(qi2, kib_ref[0])
        w_a = wq[:, idx_dim + 2 * p:idx_dim + 2 * p + 1]
        w_b = wq[:, idx_dim + 2 * p + 1:idx_dim + 2 * p + 2]
        score = score + w_a * jnp.maximum(s_a, 0.0) + w_b * jnp.maximum(s_b, 0.0)
    score = score * idx_scale + 0.0

    bits = pltpu.bitcast(score, I32)
    key = bits ^ ((bits >> 31) & 0x7FFFFFFF)
    key_ref[...] = jnp.where(admissible, key, INT_MIN)

    ones_blk = jnp.ones((LANES, LANES), MXU_DTYPE)
    rows = min(tq, SEARCH_ROWS)
    groups = [slice(c * rows, (c + 1) * rows) for c in range(tq // rows)]

    def count(rs, pred):
        acc = jnp.zeros((rows, LANES), F32)
        for c in range(nslab):
            acc = acc + jnp.where(pred(key_ref[rs, c * LANES:(c + 1) * LANES]), 1.0, 0.0)
        return jnp.sum(acc, axis=-1, keepdims=True)

    def lanes(col):
        return jnp.broadcast_to(col, (rows, LANES))

    kf = float(k_top)
    thr0 = tuple(jnp.where(count(rs, lambda x: x >= 0) >= kf, 0, INT_MIN).astype(I32) for rs in groups)

    def bit_step(i, thrs):
        bit = lax.shift_left(jnp.int32(1), jnp.asarray(30 - i, dtype=I32))
        out = []
        for rs, t in zip(groups, thrs):
            cand = t | bit
            cand_l = lanes(cand)
            out.append(jnp.where(count(rs, lambda x, c=cand_l: x >= c) >= kf, cand, t))
        return tuple(out)

    thrs = lax.fori_loop(0, 31, bit_step, thr0)

    for rs, thr in zip(groups, thrs):
        thr = lanes(jnp.maximum(thr, INT_MIN + 1))
        n_gt = count(rs, lambda x, thr=thr: x > thr)
        n_eq = count(rs, lambda x, thr=thr: x == thr)
        for c in range(nslab):
            sl = slice(c * LANES, (c + 1) * LANES)
            msk_ref[rs, sl] = jnp.where(key_ref[rs, sl] >= thr, 0.0, NEG_INF)

        has_ties = jnp.max(jnp.where(n_gt + n_eq > kf, 1.0, 0.0)) > 0.0

        @pl.when(has_ties)
        def _(rs=rs, thr=thr, n_gt=n_gt):
            need = kf - n_gt
            upper = (lax.broadcasted_iota(I32, (LANES, LANES), 0)
                     < lax.broadcasted_iota(I32, (LANES, LANES), 1)).astype(MXU_DTYPE)
            before = jnp.zeros((rows, LANES), F32)
            for c in range(nslab):
                sl = slice(c * LANES, (c + 1) * LANES)
                ks = key_ref[rs, sl]
                eq = jnp.where(ks == thr, 1.0, 0.0)
                rank = before + _dot(eq.astype(MXU_DTYPE), upper)
                keep = (ks > thr) | ((ks == thr) & (rank < need))
                msk_ref[rs, sl] = jnp.where(keep, 0.0, NEG_INF)
                before = before + _row_total(eq, ones_blk)

    dist = jnp.abs(q_pos - k_pos).astype(F32)
    for g in range(n_kv):
        cols = slice(g * head_dim, (g + 1) * head_dim)
        kg = k_ref[0, :, cols]
        vg = v_ref[0, :, cols]
        q4 = jnp.concatenate(
            [q_ref[:, (g * group + r) * head_dim:(g * group + r + 1) * head_dim] for r in range(group)],
            axis=0)
        logits = _dot_nt(q4, kg)
        for r in range(group):
            h = g * group + r
            z = logits[r * tq:(r + 1) * tq] * (attn_scale * LOG2E) - (slopes[h] * LOG2E) * dist + msk_ref[...]
            m = jnp.max(z, axis=-1, keepdims=True)
            p = jnp.exp2(z - m)
            denom = jnp.sum(p, axis=-1, keepdims=True)
            o = _dot(p.astype(MXU_DTYPE), vg) / denom
            o_ref[:, h * head_dim:(h + 1) * head_dim] = o.astype(o_ref.dtype)


def _dsa(q, qi, kiwi, k, v, kia, kib, prev, *, row0, rows, tq, tile, lk, l_valid, q_pos0, k_top, cfg):
    nb, _, a_kv = k.shape
    n_rows, a_q = q.shape
    n_q = rows // tq
    blk0 = row0 // tq
    qmap = lambda b: (blk0 + b * n_q + tile, 0)
    kmap = lambda b: (b, 0, 0)
    kern = functools.partial(
        _dsa_kernel, tq=tq, lk=lk, l_valid=l_valid, q_pos0=q_pos0, k_top=k_top,
        n_kv=cfg["n_kv"], group=cfg["group"], n_idx_heads=cfg["n_idx_heads"],
        idx_dim=cfg["idx_dim"], head_dim=cfg["head_dim"], slopes=cfg["slopes"],
        idx_scale=cfg["idx_scale"], attn_scale=cfg["attn_scale"])
    return pl.pallas_call(
        kern,
        grid=(nb,),
        in_specs=[pl.BlockSpec((tq, a_q), qmap),
                  pl.BlockSpec((tq, qi.shape[1]), qmap),
                  pl.BlockSpec((tq, LANES), qmap),
                  pl.BlockSpec((1, lk, a_kv), kmap),
                  pl.BlockSpec((1, lk, a_kv), kmap),
                  pl.BlockSpec((1, lk, LANES), kmap),
                  pl.BlockSpec((1, lk, LANES), kmap),
                  pl.BlockSpec(memory_space=pl.ANY)],
        out_specs=pl.BlockSpec((tq, a_q), qmap),
        out_shape=jax.ShapeDtypeStruct((n_rows, a_q), MXU_DTYPE),
        scratch_shapes=[pltpu.VMEM((tq, lk), I32), pltpu.VMEM((tq, lk), F32)],
        input_output_aliases={7: 0},
        name="dsa_attention",
        compiler_params=_params("arbitrary"),
    )(q, qi, kiwi, k, v, kia, kib, prev)


def _band_kernel(q_ref, k_ref, v_ref, bias_ref, prev_ref, o_ref, *, tq, win, k_pos0, heads, head_dim,
                 attn_scale):
    del prev_ref
    j = pl.program_id(2)
    start = pl.multiple_of(j * tq, tq)
    k_pos = k_pos0 + j * tq + lax.broadcasted_iota(I32, (tq, win), 1)
    before_start = jnp.where(k_pos >= 0, 0.0, NEG_INF)
    for r in range(heads):
        cols = slice(r * head_dim, (r + 1) * head_dim)
        kw = k_ref[0, pl.ds(start, win), cols]
        vw = v_ref[0, pl.ds(start, win), cols]
        z = _dot_nt(q_ref[:, cols], kw) * (attn_scale * LOG2E) + bias_ref[r] + before_start
        m = jnp.max(z, axis=-1, keepdims=True)
        p = jnp.exp2(z - m)
        denom = jnp.sum(p, axis=-1, keepdims=True)
        o = _dot(p.astype(MXU_DTYPE), vw) / denom
        o_ref[:, cols] = o.astype(o_ref.dtype)


def _band(q, kpad, vpad, bias, prev, *, row0, rows, tq, k_pos0, head_dim, attn_scale):
    nb, lk, d = kpad.shape
    n_rows = q.shape[0]
    n_heads, _, win = bias.shape
    hg = BAND_HEADS if n_heads % BAND_HEADS == 0 else 1
    wcol = hg * head_dim
    n_q = rows // tq
    blk0 = row0 // tq
    qmap = lambda b, g, j: (blk0 + b * n_q + j, g)
    kmap = lambda b, g, j: (b, 0, g)
    kern = functools.partial(_band_kernel, tq=tq, win=win, k_pos0=k_pos0, heads=hg, head_dim=head_dim,
                             attn_scale=attn_scale)
    return pl.pallas_call(
        kern,
        grid=(nb, n_heads // hg, n_q),
        in_specs=[pl.BlockSpec((tq, wcol), qmap),
                  pl.BlockSpec((1, lk, wcol), kmap),
                  pl.BlockSpec((1, lk, wcol), kmap),
                  pl.BlockSpec((hg, tq, win), lambda b, g, j: (g, 0, 0)),
                  pl.BlockSpec(memory_space=pl.ANY)],
        out_specs=pl.BlockSpec((tq, wcol), qmap),
        out_shape=jax.ShapeDtypeStruct((n_rows, d), MXU_DTYPE),
        input_output_aliases={4: 0},
        name="band_attention",
        compiler_params=_params("arbitrary", "arbitrary", "arbitrary"),
    )(q, kpad, vpad, bias, prev)


def _band_bias(table, tq, window, rel_clip):
    n_heads = table.shape[0]
    w = window + tq
    ext = w + tq - 1
    x = jnp.arange(ext, dtype=I32)
    rel = jnp.clip(window + tq - 1 - x, -rel_clip, rel_clip) + rel_clip
    e = jnp.pad(table[:, rel].astype(F32) * LOG2E, ((0, 0), (0, 1)))
    skew = jnp.broadcast_to(e[:, None, :], (n_heads, tq, ext + 1)).reshape(n_heads, tq * (ext + 1))
    skew = skew[:, :tq * ext].reshape(n_heads, tq, ext)[:, :, tq - 1:tq - 1 + w]
    qi = jnp.arange(tq, dtype=I32)[:, None]
    kj = jnp.arange(w, dtype=I32)[None, :]
    qc = qi // CHUNK
    kc = kj // CHUNK - window // CHUNK
    band = (kc <= qc) & (kc >= qc - LEFT_CHUNKS)
    return jnp.where(band[None], skew, NEG_INF)


def _router_kernel(x_ref, wh_ref, wl_ref, b_ref, rt_ref, rw_ref, cnt_ref, run_ref, *, tm, n_groups,
                   per_group):
    i = pl.program_id(0)

    @pl.when(i == 0)
    def _():
        run_ref[...] = jnp.zeros_like(run_ref)

    x = x_ref[...]
    xh = x.astype(MXU_DTYPE)
    xl = (x - xh.astype(F32)).astype(MXU_DTYPE)
    logits = _dot(xh, wh_ref[...]) + _dot(xl, wh_ref[...]) + _dot(xh, wl_ref[...]) + b_ref[...]

    lane = lax.broadcasted_iota(I32, (tm, LANES), 1)
    gl = jnp.where(lane < n_groups, logits, NEG_INF)
    gmax = jnp.max(gl, axis=-1, keepdims=True)
    gsel = jnp.min(jnp.where(gl == gmax, lane, LANES), axis=-1, keepdims=True)
    p_group = 1.0 / jnp.sum(jnp.exp(gl - gmax), axis=-1, keepdims=True)

    lo = n_groups + gsel * per_group
    el = jnp.where((lane >= lo) & (lane < lo + per_group), logits, NEG_INF)
    v1 = jnp.max(el, axis=-1, keepdims=True)
    i1 = jnp.min(jnp.where(el == v1, lane, LANES), axis=-1, keepdims=True)
    el2 = jnp.where(lane == i1, NEG_INF, el)
    v2 = jnp.max(el2, axis=-1, keepdims=True)
    i2 = jnp.min(jnp.where(el2 == v2, lane, LANES), axis=-1, keepdims=True)
    t = jnp.exp(v2 - v1)
    w1 = p_group / (1.0 + t)
    w2 = p_group * t / (1.0 + t)
    e1 = i1 - n_groups
    e2 = i2 - n_groups

    oh1 = jnp.where(lane == e1, 1.0, 0.0)
    oh2 = jnp.where(lane == e2, 1.0, 0.0)
    lower = (lax.broadcasted_iota(I32, (tm, tm), 1)
             < lax.broadcasted_iota(I32, (tm, tm), 0)).astype(MXU_DTYPE)
    pre1 = _dot(lower, oh1.astype(MXU_DTYPE))
    pre2 = _dot(lower, oh2.astype(MXU_DTYPE))
    c1 = jnp.sum(oh1, axis=0, keepdims=True)
    c2 = jnp.sum(oh2, axis=0, keepdims=True)
    run = run_ref[...]
    r1 = jnp.sum(oh1 * (pre1 + run), axis=-1, keepdims=True)
    r2 = jnp.sum(oh2 * (pre2 + run + c1), axis=-1, keepdims=True)
    run = run + c1 + c2
    run_ref[...] = run
    cnt_ref[...] = run

    fields = jnp.where(lane == 0, e1.astype(F32), jnp.where(lane == 1, e2.astype(F32), jnp.where(
        lane == 2, r1, jnp.where(lane == 3, r2, 0.0))))
    rt_ref[...] = fields.T[0:ROUTE_FIELDS]
    rw_ref[...] = jnp.where(lane == 0, w1, jnp.where(lane == 1, w2, 0.0))


def _router(x, wh, wl, bias, n_groups, per_group):
    n, d = x.shape
    tm = _pick_tile(n, LN_ROWS, LANES)
    row = lambda i: (i, 0)
    fixed = lambda i: (0, 0)
    return pl.pallas_call(
        functools.partial(_router_kernel, tm=tm, n_groups=n_groups, per_group=per_group),
        grid=(n // tm,),
        in_specs=[pl.BlockSpec((tm, d), row), pl.BlockSpec((d, LANES), fixed),
                  pl.BlockSpec((d, LANES), fixed), pl.BlockSpec((1, LANES), fixed)],
        out_specs=[pl.BlockSpec((ROUTE_FIELDS, tm), lambda i: (0, i)), pl.BlockSpec((tm, LANES), row),
                   pl.BlockSpec((1, LANES), fixed)],
        out_shape=[jax.ShapeDtypeStruct((ROUTE_FIELDS, n), F32), jax.ShapeDtypeStruct((n, LANES), F32),
                   jax.ShapeDtypeStruct((1, LANES), F32)],
        scratch_shapes=[pltpu.VMEM((1, LANES), F32)],
        name="moe_router",
        compiler_params=_params("arbitrary"),
    )(x, wh, wl, bias)


def _dispatch_kernel(dest_ref, x_ref, init_ref, xs_ref, sem, *, tm):
    del init_ref

    def row_copy(r, s):
        return pltpu.make_async_copy(x_ref.at[pl.ds(r, 1)], xs_ref.at[pl.ds(dest_ref[0, 0, s * tm + r], 1)], sem)

    def start(r, c):
        row_copy(r, 0).start(priority=0)
        row_copy(r, 1).start(priority=1)
        return c

    def wait(r, c):
        row_copy(r, 0).wait()
        row_copy(r, 1).wait()
        return c

    lax.fori_loop(0, tm, start, 0, unroll=DMA_UNROLL)
    lax.fori_loop(0, tm, wait, 0, unroll=DMA_UNROLL)


def _dispatch(x, dest3, p_rows):
    n, d = x.shape
    tm = dest3.shape[2] // 2
    init = jnp.zeros((p_rows, d), x.dtype)
    return pl.pallas_call(
        functools.partial(_dispatch_kernel, tm=tm),
        grid=(n // tm,),
        in_specs=[pl.BlockSpec((1, 1, 2 * tm), lambda i: (i, 0, 0), memory_space=pltpu.SMEM),
                  pl.BlockSpec((tm, d), lambda i: (i, 0)),
                  pl.BlockSpec(memory_space=pl.ANY)],
        out_specs=pl.BlockSpec(memory_space=pl.ANY),
        out_shape=jax.ShapeDtypeStruct((p_rows, d), x.dtype),
        scratch_shapes=[pltpu.SemaphoreType.DMA],
        input_output_aliases={2: 0},
        name="moe_dispatch",
        compiler_params=_params("arbitrary"),
    )(dest3, x, init)


def _expert_kernel(be_ref, nu_ref, xs_ref, wg_ref, wu_ref, wd_ref, ys_ref, wg_s, wu_s, wd_s):
    i = pl.program_id(0)
    used = i < nu_ref[0]
    new_expert = jnp.logical_or(i == 0, be_ref[i] != be_ref[jnp.maximum(i - 1, 0)])

    @pl.when(jnp.logical_and(used, new_expert))
    def _():
        wg_s[...] = wg_ref[0].astype(MXU_DTYPE)
        wu_s[...] = wu_ref[0].astype(MXU_DTYPE)
        wd_s[...] = wd_ref[0].astype(MXU_DTYPE)

    @pl.when(used)
    def _():
        x = xs_ref[...].astype(MXU_DTYPE)
        g = _dot(x, wg_s[...])
        u = _dot(x, wu_s[...])
        h = g * (1.0 / (1.0 + jnp.exp(-g))) * u
        ys_ref[...] = _dot(h.astype(MXU_DTYPE), wd_s[...])

    @pl.when(jnp.logical_not(used))
    def _():
        ys_ref[...] = jnp.zeros_like(ys_ref)


def _experts(xs, block_e, n_used, wg, wu, wd, tb):
    p_rows, d = xs.shape
    f = wg.shape[2]
    nb = p_rows // tb
    rmap = lambda i, be, nu: (jnp.minimum(i, nu[0] - 1), 0)
    wmap = lambda i, be, nu: (be[i], 0, 0)
    grid_spec = pltpu.PrefetchScalarGridSpec(
        num_scalar_prefetch=2,
        grid=(nb,),
        in_specs=[pl.BlockSpec((tb, d), rmap), pl.BlockSpec((1, d, f), wmap),
                  pl.BlockSpec((1, d, f), wmap), pl.BlockSpec((1, f, d), wmap)],
        out_specs=pl.BlockSpec((tb, d), lambda i, be, nu: (i, 0)),
        scratch_shapes=[pltpu.VMEM((d, f), MXU_DTYPE), pltpu.VMEM((d, f), MXU_DTYPE),
                        pltpu.VMEM((f, d), MXU_DTYPE)])
    return pl.pallas_call(
        _expert_kernel,
        grid_spec=grid_spec,
        out_shape=jax.ShapeDtypeStruct((p_rows, d), F32),
        name="moe_experts",
        compiler_params=_params("arbitrary"),
    )(block_e, n_used, xs, wg, wu, wd)


def _combine_kernel(dest_ref, dnext_ref, x_ref, rw_ref, g_ref, b_ref, ys_ref, o_ref, ob_ref, buf, sem, *,
                    tm, alpha):
    i = pl.program_id(0)
    slot = i % 2

    def row_copy(dref, r, s, sl):
        return pltpu.make_async_copy(ys_ref.at[pl.ds(dref[0, 0, s * tm + r], 1)], buf.at[sl, s, pl.ds(r, 1)],
                                     sem.at[sl])

    def issue(dref, sl):
        def body(r, c):
            row_copy(dref, r, 0, sl).start(priority=0)
            row_copy(dref, r, 1, sl).start(priority=1)
            return c
        lax.fori_loop(0, tm, body, 0, unroll=DMA_UNROLL)

    @pl.when(i == 0)
    def _():
        issue(dest_ref, 0)

    @pl.when(i + 1 < pl.num_programs(0))
    def _():
        issue(dnext_ref, 1 - slot)

    def wait(r, c):
        row_copy(dest_ref, r, 0, slot).wait()
        row_copy(dest_ref, r, 1, slot).wait()
        return c

    lax.fori_loop(0, tm, wait, 0, unroll=DMA_UNROLL)
    rw = rw_ref[...]
    m = rw[:, 0:1] * buf[slot, 0] + rw[:, 1:2] * buf[slot, 1]
    out = _layer_norm(alpha * x_ref[...] + m, g_ref[...], b_ref[...])
    o_ref[...] = out
    ob_ref[...] = out.astype(ob_ref.dtype)


def _combine(x, ys, dest3, rw, g, b, alpha):
    n, d = x.shape
    n_steps = dest3.shape[0]
    tm = dest3.shape[2] // 2
    row = lambda i: (i, 0)
    fixed = lambda i: (0, 0)
    return pl.pallas_call(
        functools.partial(_combine_kernel, tm=tm, alpha=alpha),
        grid=(n_steps,),
        in_specs=[pl.BlockSpec((1, 1, 2 * tm), lambda i: (i, 0, 0), memory_space=pltpu.SMEM),
                  pl.BlockSpec((1, 1, 2 * tm), lambda i: (jnp.minimum(i + 1, n_steps - 1), 0, 0),
                               memory_space=pltpu.SMEM),
                  pl.BlockSpec((tm, d), row), pl.BlockSpec((tm, LANES), row),
                  pl.BlockSpec((1, d), fixed), pl.BlockSpec((1, d), fixed),
                  pl.BlockSpec(memory_space=pl.ANY)],
        out_specs=[pl.BlockSpec((tm, d), row), pl.BlockSpec((tm, d), row)],
        out_shape=[jax.ShapeDtypeStruct((n, d), F32), jax.ShapeDtypeStruct((n, d), MXU_DTYPE)],
        scratch_shapes=[pltpu.VMEM((2, 2, tm, d), F32), pltpu.SemaphoreType.DMA((2,))],
        name="moe_combine_norm",
        compiler_params=_params("arbitrary"),
    )(dest3, dest3, x, rw, g.reshape(1, d), b.reshape(1, d), ys)


def _moe(x, w_group, b_group, w_router, b_router, wg, wu, wd, ln_g, ln_b, alpha):
    n, d = x.shape
    tb = MOE_BLOCK_ROWS
    n_groups = w_group.shape[1]
    n_experts = w_router.shape[1]
    per_group = n_experts // n_groups
    pad = LANES - n_groups - n_experts
    w_all = jnp.pad(jnp.concatenate([w_group, w_router], axis=1), ((0, 0), (0, pad)))
    b_all = jnp.pad(jnp.concatenate([b_group, b_router]), (0, pad)).reshape(1, LANES)
    wh = w_all.astype(MXU_DTYPE)
    wl = (w_all - wh.astype(F32)).astype(MXU_DTYPE)
    rt, rw, cnt = _router(x, wh, wl, b_all, n_groups, per_group)

    counts = cnt[0, :n_experts].astype(I32)
    padded = (counts + tb - 1) // tb * tb
    pend = jnp.cumsum(padded)
    poff = pend - padded
    dest = poff[rt[0:2].astype(I32)] + rt[2:4].astype(I32)
    tm = _pick_tile(n, DMA_ROWS, 16)
    dest3 = dest.reshape(2, n // tm, tm).transpose(1, 0, 2).reshape(n // tm, 1, 2 * tm)
    p_rows = (2 * n + n_experts * (tb - 1) + tb - 1) // tb * tb
    starts = jnp.arange(p_rows // tb, dtype=I32) * tb
    block_e = jnp.minimum(jnp.sum(pend[None, :] <= starts[:, None], axis=1), n_experts - 1).astype(I32)
    n_used = (pend[-1:] // tb).astype(I32)

    xs = _dispatch(x, dest3, p_rows)
    ys = _experts(xs, block_e, n_used, wg, wu, wd, tb)
    return _combine(x, ys, dest3, rw, ln_g, ln_b, alpha)


def kernel(x_prompt, x_sample, cache_a_k, cache_a_v, cache_a_idx, cache_b_k, cache_b_v, a_w_in, a_w_o, b_w_q, b_w_kv, b_w_o, b_rel_bias, ln_mix_g, ln_mix_b, ln_ffn_g, ln_ffn_b, moe_w_group, moe_b_group, moe_w_router, moe_b_router, moe_w_gate, moe_w_up, moe_w_down):
    nb_p, t_p, d = x_prompt.shape
    nb_s, t_s, _ = x_sample.shape
    n_a, _, past, n_kv, head_dim = cache_a_k.shape
    idx_dim = cache_a_idx.shape[-1]
    depth = ln_mix_g.shape[0]
    n_heads = d // head_dim
    a_q = n_heads * head_dim
    a_kv = n_kv * head_dim
    n_idx_heads = (a_w_in.shape[-1] - a_q - 2 * a_kv - idx_dim) // (idx_dim + 1)
    a_qi = n_idx_heads * idx_dim
    window = LEFT_CHUNKS * CHUNK
    rel_clip = (b_rel_bias.shape[-1] - 1) // 2
    alpha = (2 * depth) ** 0.25
    attn_scale = head_dim ** -0.5
    np_rows = nb_p * t_p
    ns_rows = nb_s * t_s
    n = np_rows + ns_rows
    assert 2 * idx_dim == LANES and n_idx_heads % 2 == 0 and n_idx_heads <= LANES - idx_dim
    assert t_p % CHUNK == 0 and past % CHUNK == 0 and t_s <= CHUNK and window % t_s == 0
    assert cache_b_k.shape[1] == window

    cfg = dict(n_kv=n_kv, group=n_heads // n_kv, n_idx_heads=n_idx_heads, idx_dim=idx_dim,
               head_dim=head_dim, attn_scale=attn_scale,
               idx_scale=idx_dim ** -0.5 * n_idx_heads ** -0.5,
               slopes=tuple(2.0 ** (-8.0 * (h + 1) / n_heads) for h in range(n_heads)))

    tq_a = _pick_tile(t_p, DSA_TILE, LANES)
    tq_b = _pick_tile(t_p, BAND_TILE, CHUNK)
    l_s = past + t_s
    lp_s = -(-l_s // LANES) * LANES

    x = jnp.concatenate([x_prompt.reshape(np_rows, d), x_sample.reshape(ns_rows, d)], axis=0)
    xb = x
    new_k, new_v, new_i = [], [], []
    kb = vb = None
    for layer in range(depth):
        if layer < n_a:
            w = a_w_in[layer]
            o3 = a_q + 2 * a_kv
            w_kiwi = jnp.pad(w[:, o3 + a_qi:], ((0, 0), (0, LANES - idx_dim - n_idx_heads)))
            q = _mm(xb, w[:, :a_q].astype(MXU_DTYPE), MXU_DTYPE)
            k_new, k_mx = _mm_dual(xb, w[:, a_q:a_q + a_kv].astype(MXU_DTYPE))
            v_new, v_mx = _mm_dual(xb, w[:, a_q + a_kv:o3].astype(MXU_DTYPE))
            qi = _mm(xb, w[:, o3:o3 + a_qi].astype(MXU_DTYPE), MXU_DTYPE)
            kiwi = _mm(xb, w_kiwi.astype(MXU_DTYPE), F32)
            ki_new = kiwi[:, :idx_dim]
            ki_mx = ki_new.astype(MXU_DTYPE)
            new_k.append(k_new)
            new_v.append(v_new)
            new_i.append(ki_new)

            def key_side(arr, cache, width):
                kp = arr[:np_rows].reshape(nb_p, t_p, width)
                ks = jnp.concatenate([cache.astype(MXU_DTYPE).reshape(nb_s, past, width),
                                      arr[np_rows:].reshape(nb_s, t_s, width)], axis=1)
                return kp, jnp.pad(ks, ((0, 0), (0, lp_s - l_s), (0, 0)))

            k_p, k_s = key_side(k_mx, cache_a_k[layer], a_kv)
            v_p, v_s = key_side(v_mx, cache_a_v[layer], a_kv)
            i_p, i_s = key_side(ki_mx, cache_a_idx[layer], idx_dim)
            zpad = ((0, 0), (0, 0), (0, LANES - idx_dim))
            zpad_front = ((0, 0), (0, 0), (LANES - idx_dim, 0))
            keys_p = (k_p, v_p, jnp.pad(i_p, zpad), jnp.pad(i_p, zpad_front))
            keys_s = (k_s, v_s, jnp.pad(i_s, zpad), jnp.pad(i_s, zpad_front))
            o = jnp.zeros((n, a_q), MXU_DTYPE)
            for u in range(t_p // tq_a):
                o = _dsa(q, qi, kiwi, *keys_p, o, row0=0, rows=t_p, tq=tq_a, tile=u, lk=(u + 1) * tq_a,
                         l_valid=t_p, q_pos0=u * tq_a, k_top=min(TOPK_MAX, t_p // 4), cfg=cfg)
            o = _dsa(q, qi, kiwi, *keys_s, o, row0=np_rows, rows=t_s, tq=t_s, tile=0, lk=lp_s,
                     l_valid=l_s, q_pos0=past, k_top=min(TOPK_MAX, l_s // 4), cfg=cfg)
            w_o = a_w_o[layer]
        else:
            jb = layer - n_a
            q = _mm(xb, b_w_q[jb].astype(MXU_DTYPE), MXU_DTYPE)
            table = b_rel_bias[jb]
            o = jnp.zeros((n, d), MXU_DTYPE)
            o = _band(q, kb[0], vb[0], _band_bias(table, tq_b, window, rel_clip), o, row0=0, rows=t_p,
                      tq=tq_b, k_pos0=-window, head_dim=head_dim, attn_scale=attn_scale)
            o = _band(q, kb[1], vb[1], _band_bias(table, t_s, window, rel_clip), o, row0=np_rows, rows=t_s,
                      tq=t_s, k_pos0=past - window, head_dim=head_dim, attn_scale=attn_scale)
            w_o = b_w_o[jb]
        x, xb = _mm_ln(o, w_o.astype(MXU_DTYPE), x, ln_mix_g[layer], ln_mix_b[layer], alpha)
        x, xb = _moe(x, moe_w_group[layer], moe_b_group[layer], moe_w_router[layer], moe_b_router[layer],
                     moe_w_gate[layer], moe_w_up[layer], moe_w_down[layer], ln_ffn_g[layer], ln_ffn_b[layer],
                     alpha)
        if layer == n_a - 1:
            kb_new, kb_mx = _mm_dual(xb, b_w_kv[:, :d].astype(MXU_DTYPE))
            vb_new, vb_mx = _mm_dual(xb, b_w_kv[:, d:].astype(MXU_DTYPE))

            def band_ctx(arr, cache):
                ctx_p = jnp.pad(arr[:np_rows].reshape(nb_p, t_p, d), ((0, 0), (window, 0), (0, 0)))
                ctx_s = jnp.concatenate([cache.astype(MXU_DTYPE).reshape(nb_s, window, d),
                                         arr[np_rows:].reshape(nb_s, t_s, d)], axis=1)
                return ctx_p, ctx_s

            kb = band_ctx(kb_mx, cache_b_k)
            vb = band_ctx(vb_mx, cache_b_v)

    keep = min(window, t_p)

    def split(arrs, shape_tail):
        st = jnp.stack(arrs)
        return (st[:, :np_rows].reshape((n_a, nb_p, t_p) + shape_tail),
                st[:, np_rows:].reshape((n_a, nb_s, t_s) + shape_tail))

    a_k_p, a_k_s = split(new_k, (n_kv, head_dim))
    a_v_p, a_v_s = split(new_v, (n_kv, head_dim))
    a_i_p, a_i_s = split(new_i, (idx_dim,))
    heads_b = (n_heads, head_dim)
    b_k_p = kb_new[:np_rows].reshape((nb_p, t_p) + heads_b)[:, t_p - keep:]
    b_v_p = vb_new[:np_rows].reshape((nb_p, t_p) + heads_b)[:, t_p - keep:]
    b_k_s = kb_new[np_rows:].reshape((nb_s, t_s) + heads_b)
    b_v_s = vb_new[np_rows:].reshape((nb_s, t_s) + heads_b)
    return (x[:np_rows].reshape(nb_p, t_p, d), x[np_rows:].reshape(nb_s, t_s, d),
            a_k_p, a_v_p, a_i_p, b_k_p, b_v_p, a_k_s, a_v_s, a_i_s, b_k_s, b_v_s)
```

```python
import functools
import math

import jax
import jax.numpy as jnp
from jax import lax
from jax.experimental import pallas as pl
from jax.experimental.pallas import tpu as pltpu

CHUNK = 64
TOPK_MAX = 256
LEFT_CHUNKS = 8
LN_EPS = 1e-5

LANES = 128
VMEM_LIMIT_BYTES = 56 * 1024 * 1024
MXU_DTYPE = jnp.bfloat16

MM_ROWS = 768
LN_ROWS = 384
DMA_ROWS = 256
DMA_UNROLL = 8
MOE_BLOCK_ROWS = 512
ROUTE_FIELDS = 8
DSA_TILE = 256
SEARCH_ROWS = 128
BAND_TILE = 256
BAND_HEADS = 8

F32 = jnp.float32
I32 = jnp.int32
NEG_INF = float("-inf")
INT_MIN = -(2 ** 31)
LOG2E = math.log2(math.e)


def _pick_tile(n, cap, mult):
    best = None
    for t in range(mult, min(n, cap) + 1, mult):
        if n % t == 0:
            best = t
    if best is None:
        raise ValueError(f"no tile for n={n} cap={cap} mult={mult}")
    return best


def _params(*sem):
    return pltpu.CompilerParams(dimension_semantics=sem, vmem_limit_bytes=VMEM_LIMIT_BYTES)


def _dot(a, b):
    return jnp.dot(a, b, preferred_element_type=F32)


def _dot_nt(a, b):
    return lax.dot_general(a, b, (((1,), (1,)), ((), ())), preferred_element_type=F32)


def _layer_norm(z, g, b):
    mu = jnp.mean(z, axis=-1, keepdims=True)
    zc = z - mu
    var = jnp.mean(zc * zc, axis=-1, keepdims=True)
    return zc * lax.rsqrt(var + LN_EPS) * g + b


def _mm_kernel(x_ref, w_ref, o_ref):
    o_ref[...] = _dot(x_ref[...].astype(MXU_DTYPE), w_ref[...]).astype(o_ref.dtype)


def _mm(x, w, out_dtype):
    n, k = x.shape
    m = w.shape[1]
    tm = _pick_tile(n, MM_ROWS, 16)
    tn = _pick_tile(m, 2048, LANES)
    return pl.pallas_call(
        _mm_kernel,
        grid=(m // tn, n // tm),
        in_specs=[pl.BlockSpec((tm, k), lambda j, i: (i, 0)),
                  pl.BlockSpec((k, tn), lambda j, i: (0, j))],
        out_specs=pl.BlockSpec((tm, tn), lambda j, i: (i, j)),
        out_shape=jax.ShapeDtypeStruct((n, m), out_dtype),
        name="projection",
        compiler_params=_params("arbitrary", "arbitrary"),
    )(x, w)


def _mm_dual_kernel(x_ref, w_ref, o_ref, ob_ref):
    y = _dot(x_ref[...].astype(MXU_DTYPE), w_ref[...])
    o_ref[...] = y
    ob_ref[...] = y.astype(ob_ref.dtype)


def _mm_dual(x, w):
    n, k = x.shape
    m = w.shape[1]
    tm = _pick_tile(n, MM_ROWS, 16)
    return pl.pallas_call(
        _mm_dual_kernel,
        grid=(n // tm,),
        in_specs=[pl.BlockSpec((tm, k), lambda i: (i, 0)),
                  pl.BlockSpec((k, m), lambda i: (0, 0))],
        out_specs=[pl.BlockSpec((tm, m), lambda i: (i, 0)), pl.BlockSpec((tm, m), lambda i: (i, 0))],
        out_shape=[jax.ShapeDtypeStruct((n, m), F32), jax.ShapeDtypeStruct((n, m), MXU_DTYPE)],
        name="projection_dual",
        compiler_params=_params("arbitrary"),
    )(x, w)


def _mm_ln_kernel(a_ref, w_ref, x_ref, g_ref, b_ref, o_ref, ob_ref, *, alpha):
    y = _dot(a_ref[...], w_ref[...])
    out = _layer_norm(alpha * x_ref[...] + y, g_ref[...], b_ref[...])
    o_ref[...] = out
    ob_ref[...] = out.astype(ob_ref.dtype)


def _mm_ln(a, w, x, g, b, alpha):
    n, k = a.shape
    d = w.shape[1]
    tm = _pick_tile(n, LN_ROWS, 16)
    row = lambda i: (i, 0)
    fixed = lambda i: (0, 0)
    return pl.pallas_call(
        functools.partial(_mm_ln_kernel, alpha=alpha),
        grid=(n // tm,),
        in_specs=[pl.BlockSpec((tm, k), row), pl.BlockSpec((k, d), fixed),
                  pl.BlockSpec((tm, d), row), pl.BlockSpec((1, d), fixed),
                  pl.BlockSpec((1, d), fixed)],
        out_specs=[pl.BlockSpec((tm, d), row), pl.BlockSpec((tm, d), row)],
        out_shape=[jax.ShapeDtypeStruct((n, d), F32), jax.ShapeDtypeStruct((n, d), MXU_DTYPE)],
        name="out_projection_norm",
        compiler_params=_params("arbitrary"),
    )(a, w, x, g.reshape(1, d), b.reshape(1, d))


def _row_total(acc, ones_blk):
    return _dot(acc.astype(MXU_DTYPE), ones_blk)


def _dsa_kernel(q_ref, qi_ref, wq_ref, k_ref, v_ref, kia_ref, kib_ref, prev_ref, o_ref,
                key_ref, msk_ref, *, tq, lk, l_valid, q_pos0, k_top, n_kv, group,
                n_idx_heads, idx_dim, head_dim, slopes, idx_scale, attn_scale):
    del prev_ref
    nslab = lk // LANES
    q_pos = q_pos0 + lax.broadcasted_iota(I32, (tq, lk), 0)
    k_pos = lax.broadcasted_iota(I32, (tq, lk), 1)
    admissible = ((k_pos // CHUNK) <= (q_pos // CHUNK)) & (k_pos < l_valid)

    wq = wq_ref[...]
    score = jnp.zeros((tq, lk), F32)
    for p in range(n_idx_heads // 2):
        qi2 = qi_ref[:, p * 2 * idx_dim:(p + 1) * 2 * idx_dim]
        s_a = _dot_nt(qi2, kia_ref[0])
        s_b = _dot_nt(qi2, kib_ref[0])
        w_a = wq[:, idx_dim + 2 * p:idx_dim + 2 * p + 1]
        w_b = wq[:, idx_dim + 2 * p + 1:idx_dim + 2 * p + 2]
        score = score + w_a * jnp.maximum(s_a, 0.0) + w_b * jnp.maximum(s_b, 0.0)
    score = score * idx_scale + 0.0

    bits = pltpu.bitcast(score, I32)
    key = bits ^ ((bits >> 31) & 0x7FFFFFFF)
    key_ref[...] = jnp.where(admissible, key, INT_MIN)

    ones_blk = jnp.ones((LANES, LANES), MXU_DTYPE)
    rows = min(tq, SEARCH_ROWS)
    groups = [slice(c * rows, (c + 1) * rows) for c in range(tq // rows)]

    def count(rs, pred):
        acc = jnp.zeros((rows, LANES), F32)
        for c in range(nslab):
            acc = acc + jnp.where(pred(key_ref[rs, c * LANES:(c + 1) * LANES]), 1.0, 0.0)
        return jnp.sum(acc, axis=-1, keepdims=True)

    def lanes(col):
        return jnp.broadcast_to(col, (rows, LANES))

    kf = float(k_top)
    thr0 = tuple(jnp.where(count(rs, lambda x: x >= 0) >= kf, 0, INT_MIN).astype(I32) for rs in groups)

    def bit_step(i, thrs):
        bit = lax.shift_left(jnp.int32(1), jnp.asarray(30 - i, dtype=I32))
        out = []
        for rs, t in zip(groups, thrs):
            cand = t | bit
            cand_l = lanes(cand)
            out.append(jnp.where(count(rs, lambda x, c=cand_l: x >= c) >= kf, cand, t))
        return tuple(out)

    thrs = lax.fori_loop(0, 31, bit_step, thr0)

    for rs, thr in zip(groups, thrs):
        thr = lanes(jnp.maximum(thr, INT_MIN + 1))
        n_gt = count(rs, lambda x, thr=thr: x > thr)
        n_eq = count(rs, lambda x, thr=thr: x == thr)
        for c in range(nslab):
            sl = slice(c * LANES, (c + 1) * LANES)
            msk_ref[rs, sl] = jnp.where(key_ref[rs, sl] >= thr, 0.0, NEG_INF)

        has_ties = jnp.max(jnp.where(n_gt + n_eq > kf, 1.0, 0.0)) > 0.0

        @pl.when(has_ties)
        def _(rs=rs, thr=thr, n_gt=n_gt):
            need = kf - n_gt
            upper = (lax.broadcasted_iota(I32, (LANES, LANES), 0)
                     < lax.broadcasted_iota(I32, (LANES, LANES), 1)).astype(MXU_DTYPE)
            before = jnp.zeros((rows, LANES), F32)
            for c in range(nslab):
                sl = slice(c * LANES, (c + 1) * LANES)
                ks = key_ref[rs, sl]
                eq = jnp.where(ks == thr, 1.0, 0.0)
                rank = before + _dot(eq.astype(MXU_DTYPE), upper)
                keep = (ks > thr) | ((ks == thr) & (rank < need))
                msk_ref[rs, sl] = jnp.where(keep, 0.0, NEG_INF)
                before = before + _row_total(eq, ones_blk)

    dist = jnp.abs(q_pos - k_pos).astype(F32)
    for g in range(n_kv):
        cols = slice(g * head_dim, (g + 1) * head_dim)
        kg = k_ref[0, :, cols]
        vg = v_ref[0, :, cols]
        q4 = jnp.concatenate(
            [q_ref[:, (g * group + r) * head_dim:(g * group + r + 1) * head_dim] for r in range(group)],
            axis=0)
        logits = _dot_nt(q4, kg)
        for r in range(group):
            h = g * group + r
            z = logits[r * tq:(r + 1) * tq] * (attn_scale * LOG2E) - (slopes[h] * LOG2E) * dist + msk_ref[...]
            m = jnp.max(z, axis=-1, keepdims=True)
            p = jnp.exp2(z - m)
            denom = jnp.sum(p, axis=-1, keepdims=True)
            o = _dot(p.astype(MXU_DTYPE), vg) / denom
            o_ref[:, h * head_dim:(h + 1) * head_dim] = o.astype(o_ref.dtype)


def _dsa(q, qi, kiwi, k, v, kia, kib, prev, *, row0, rows, tq, tile, lk, l_valid, q_pos0, k_top, cfg):
    nb, _, a_kv = k.shape
    n_rows, a_q = q.shape
    n_q = rows // tq
    blk0 = row0 // tq
    qmap = lambda b: (blk0 + b * n_q + tile, 0)
    kmap = lambda b: (b, 0, 0)
    kern = functools.partial(
        _dsa_kernel, tq=tq, lk=lk, l_valid=l_valid, q_pos0=q_pos0, k_top=k_top,
        n_kv=cfg["n_kv"], group=cfg["group"], n_idx_heads=cfg["n_idx_heads"],
        idx_dim=cfg["idx_dim"], head_dim=cfg["head_dim"], slopes=cfg["slopes"],
        idx_scale=cfg["idx_scale"], attn_scale=cfg["attn_scale"])
    return pl.pallas_call(
        kern,
        grid=(nb,),
        in_specs=[pl.BlockSpec((tq, a_q), qmap),
                  pl.BlockSpec((tq, qi.shape[1]), qmap),
                  pl.BlockSpec((tq, LANES), qmap),
                  pl.BlockSpec((1, lk, a_kv), kmap),
                  pl.BlockSpec((1, lk, a_kv), kmap),
                  pl.BlockSpec((1, lk, LANES), kmap),
                  pl.BlockSpec((1, lk, LANES), kmap),
                  pl.BlockSpec(memory_space=pl.ANY)],
        out_specs=pl.BlockSpec((tq, a_q), qmap),
        out_shape=jax.ShapeDtypeStruct((n_rows, a_q), MXU_DTYPE),
        scratch_shapes=[pltpu.VMEM((tq, lk), I32), pltpu.VMEM((tq, lk), F32)],
        input_output_aliases={7: 0},
        name="dsa_attention",
        compiler_params=_params("arbitrary"),
    )(q, qi, kiwi, k, v, kia, kib, prev)


def _band_kernel(q_ref, k_ref, v_ref, bias_ref, prev_ref, o_ref, *, tq, win, k_pos0, heads, head_dim,
                 attn_scale):
    del prev_ref
    j = pl.program_id(2)
    start = pl.multiple_of(j * tq, tq)
    k_pos = k_pos0 + j * tq + lax.broadcasted_iota(I32, (tq, win), 1)
    before_start = jnp.where(k_pos >= 0, 0.0, NEG_INF)
    for r in range(heads):
        cols = slice(r * head_dim, (r + 1) * head_dim)
        kw = k_ref[0, pl.ds(start, win), cols]
        vw = v_ref[0, pl.ds(start, win), cols]
        z = _dot_nt(q_ref[:, cols], kw) * (attn_scale * LOG2E) + bias_ref[r] + before_start
        m = jnp.max(z, axis=-1, keepdims=True)
        p = jnp.exp2(z - m)
        denom = jnp.sum(p, axis=-1, keepdims=True)
        o = _dot(p.astype(MXU_DTYPE), vw) / denom
        o_ref[:, cols] = o.astype(o_ref.dtype)


def _band(q, kpad, vpad, bias, prev, *, row0, rows, tq, k_pos0, head_dim, attn_scale):
    nb, lk, d = kpad.shape
    n_rows = q.shape[0]
    n_heads, _, win = bias.shape
    hg = BAND_HEADS if n_heads % BAND_HEADS == 0 else 1
    wcol = hg * head_dim
    n_q = rows // tq
    blk0 = row0 // tq
    qmap = lambda b, g, j: (blk0 + b * n_q + j, g)
    kmap = lambda b, g, j: (b, 0, g)
    kern = functools.partial(_band_kernel, tq=tq, win=win, k_pos0=k_pos0, heads=hg, head_dim=head_dim,
                             attn_scale=attn_scale)
    return pl.pallas_call(
        kern,
        grid=(nb, n_heads // hg, n_q),
        in_specs=[pl.BlockSpec((tq, wcol), qmap),
                  pl.BlockSpec((1, lk, wcol), kmap),
                  pl.BlockSpec((1, lk, wcol), kmap),
                  pl.BlockSpec((hg, tq, win), lambda b, g, j: (g, 0, 0)),
                  pl.BlockSpec(memory_space=pl.ANY)],
        out_specs=pl.BlockSpec((tq, wcol), qmap),
        out_shape=jax.ShapeDtypeStruct((n_rows, d), MXU_DTYPE),
        input_output_aliases={4: 0},
        name="band_attention",
        compiler_params=_params("arbitrary", "arbitrary", "arbitrary"),
    )(q, kpad, vpad, bias, prev)


def _band_bias(table, tq, window, rel_clip):
    n_heads = table.shape[0]
    w = window + tq
    ext = w + tq - 1
    x = jnp.arange(ext, dtype=I32)
    rel = jnp.clip(window + tq - 1 - x, -rel_clip, rel_clip) + rel_clip
    e = jnp.pad(table[:, rel].astype(F32) * LOG2E, ((0, 0), (0, 1)))
    skew = jnp.broadcast_to(e[:, None, :], (n_heads, tq, ext + 1)).reshape(n_heads, tq * (ext + 1))
    skew = skew[:, :tq * ext].reshape(n_heads, tq, ext)[:, :, tq - 1:tq - 1 + w]
    qi = jnp.arange(tq, dtype=I32)[:, None]
    kj = jnp.arange(w, dtype=I32)[None, :]
    qc = qi // CHUNK
    kc = kj // CHUNK - window // CHUNK
    band = (kc <= qc) & (kc >= qc - LEFT_CHUNKS)
    return jnp.where(band[None], skew, NEG_INF)


def _router_kernel(x_ref, wh_ref, wl_ref, b_ref, rt_ref, rw_ref, cnt_ref, run_ref, *, tm, n_groups,
                   per_group):
    i = pl.program_id(0)

    @pl.when(i == 0)
    def _():
        run_ref[...] = jnp.zeros_like(run_ref)

    x = x_ref[...]
    xh = x.astype(MXU_DTYPE)
    xl = (x - xh.astype(F32)).astype(MXU_DTYPE)
    logits = _dot(xh, wh_ref[...]) + _dot(xl, wh_ref[...]) + _dot(xh, wl_ref[...]) + b_ref[...]

    lane = lax.broadcasted_iota(I32, (tm, LANES), 1)
    gl = jnp.where(lane < n_groups, logits, NEG_INF)
    gmax = jnp.max(gl, axis=-1, keepdims=True)
    gsel = jnp.min(jnp.where(gl == gmax, lane, LANES), axis=-1, keepdims=True)
    p_group = 1.0 / jnp.sum(jnp.exp(gl - gmax), axis=-1, keepdims=True)

    lo = n_groups + gsel * per_group
    el = jnp.where((lane >= lo) & (lane < lo + per_group), logits, NEG_INF)
    v1 = jnp.max(el, axis=-1, keepdims=True)
    i1 = jnp.min(jnp.where(el == v1, lane, LANES), axis=-1, keepdims=True)
    el2 = jnp.where(lane == i1, NEG_INF, el)
    v2 = jnp.max(el2, axis=-1, keepdims=True)
    i2 = jnp.min(jnp.where(el2 == v2, lane, LANES), axis=-1, keepdims=True)
    t = jnp.exp(v2 - v1)
    w1 = p_group / (1.0 + t)
    w2 = p_group * t / (1.0 + t)
    e1 = i1 - n_groups
    e2 = i2 - n_groups

    oh1 = jnp.where(lane == e1, 1.0, 0.0)
    oh2 = jnp.where(lane == e2, 1.0, 0.0)
    lower = (lax.broadcasted_iota(I32, (tm, tm), 1)
             < lax.broadcasted_iota(I32, (tm, tm), 0)).astype(MXU_DTYPE)
    pre1 = _dot(lower, oh1.astype(MXU_DTYPE))
    pre2 = _dot(lower, oh2.astype(MXU_DTYPE))
    c1 = jnp.sum(oh1, axis=0, keepdims=True)
    c2 = jnp.sum(oh2, axis=0, keepdims=True)
    run = run_ref[...]
    r1 = jnp.sum(oh1 * (pre1 + run), axis=-1, keepdims=True)
    r2 = jnp.sum(oh2 * (pre2 + run + c1), axis=-1, keepdims=True)
    run = run + c1 + c2
    run_ref[...] = run
    cnt_ref[...] = run

    fields = jnp.where(lane == 0, e1.astype(F32), jnp.where(lane == 1, e2.astype(F32), jnp.where(
        lane == 2, r1, jnp.where(lane == 3, r2, 0.0))))
    rt_ref[...] = fields.T[0:ROUTE_FIELDS]
    rw_ref[...] = jnp.where(lane == 0, w1, jnp.where(lane == 1, w2, 0.0))


def _router(x, wh, wl, bias, n_groups, per_group):
    n, d = x.shape
    tm = _pick_tile(n, LN_ROWS, LANES)
    row = lambda i: (i, 0)
    fixed = lambda i: (0, 0)
    return pl.pallas_call(
        functools.partial(_router_kernel, tm=tm, n_groups=n_groups, per_group=per_group),
        grid=(n // tm,),
        in_specs=[pl.BlockSpec((tm, d), row), pl.BlockSpec((d, LANES), fixed),
                  pl.BlockSpec((d, LANES), fixed), pl.BlockSpec((1, LANES), fixed)],
        out_specs=[pl.BlockSpec((ROUTE_FIELDS, tm), lambda i: (0, i)), pl.BlockSpec((tm, LANES), row),
                   pl.BlockSpec((1, LANES), fixed)],
        out_shape=[jax.ShapeDtypeStruct((ROUTE_FIELDS, n), F32), jax.ShapeDtypeStruct((n, LANES), F32),
                   jax.ShapeDtypeStruct((1, LANES), F32)],
        scratch_shapes=[pltpu.VMEM((1, LANES), F32)],
        name="moe_router",
        compiler_params=_params("arbitrary"),
    )(x, wh, wl, bias)


def _dispatch_kernel(pend_ref, dest_ref, x_ref, xs_ref, zero_ref, sem, zsem, *, tm, tb, n_experts):
    @pl.when(pl.program_id(0) == 0)
    def _():
        zero_ref[...] = jnp.zeros_like(zero_ref)

        def block_copy(e):
            first = pl.multiple_of(jnp.maximum(pend_ref[e] - tb, 0), tb)
            return pltpu.make_async_copy(zero_ref, xs_ref.at[pl.ds(first, tb)], zsem)

        def nonempty(e):
            return pend_ref[e] > (pend_ref[e - 1] if e else 0)

        for e in range(n_experts):
            pl.when(nonempty(e))(lambda e=e: block_copy(e).start())
        for e in range(n_experts):
            pl.when(nonempty(e))(lambda e=e: block_copy(e).wait())

        def tail_copy(j):
            return pltpu.make_async_copy(zero_ref, xs_ref.at[pl.ds(pl.multiple_of(j * tb, tb), tb)], zsem)

        n_used = pend_ref[n_experts - 1] // tb
        n_blocks = xs_ref.shape[0] // tb
        lax.fori_loop(n_used, n_blocks, lambda j, c: (tail_copy(j).start(), c)[1], 0)
        lax.fori_loop(n_used, n_blocks, lambda j, c: (tail_copy(j).wait(), c)[1], 0)

    def row_copy(r, s):
        return pltpu.make_async_copy(x_ref.at[pl.ds(r, 1)], xs_ref.at[pl.ds(dest_ref[0, 0, s * tm + r], 1)], sem)

    def start(r, c):
        row_copy(r, 0).start(priority=0)
        row_copy(r, 1).start(priority=1)
        return c

    def wait(r, c):
        row_copy(r, 0).wait()
        row_copy(r, 1).wait()
        return c

    lax.fori_loop(0, tm, start, 0, unroll=DMA_UNROLL)
    lax.fori_loop(0, tm, wait, 0, unroll=DMA_UNROLL)


def _dispatch(x, dest3, pend, p_rows, tb):
    n, d = x.shape
    tm = dest3.shape[2] // 2
    grid_spec = pltpu.PrefetchScalarGridSpec(
        num_scalar_prefetch=1,
        grid=(n // tm,),
        in_specs=[pl.BlockSpec((1, 1, 2 * tm), lambda i, pe: (i, 0, 0), memory_space=pltpu.SMEM),
                  pl.BlockSpec((tm, d), lambda i, pe: (i, 0))],
        out_specs=pl.BlockSpec(memory_space=pl.ANY),
        scratch_shapes=[pltpu.VMEM((tb, d), x.dtype), pltpu.SemaphoreType.DMA, pltpu.SemaphoreType.DMA])
    return pl.pallas_call(
        functools.partial(_dispatch_kernel, tm=tm, tb=tb, n_experts=pend.shape[0]),
        grid_spec=grid_spec,
        out_shape=jax.ShapeDtypeStruct((p_rows, d), x.dtype),
        name="moe_dispatch",
        compiler_params=_params("arbitrary"),
    )(pend, dest3, x)


def _expert_kernel(be_ref, nu_ref, xs_ref, wg_ref, wu_ref, wd_ref, ys_ref, wg_s, wu_s, wd_s):
    i = pl.program_id(0)
    used = i < nu_ref[0]
    new_expert = jnp.logical_or(i == 0, be_ref[i] != be_ref[jnp.maximum(i - 1, 0)])

    @pl.when(jnp.logical_and(used, new_expert))
    def _():
        wg_s[...] = wg_ref[0, 0].astype(MXU_DTYPE)
        wu_s[...] = wu_ref[0, 0].astype(MXU_DTYPE)
        wd_s[...] = wd_ref[0, 0].astype(MXU_DTYPE)

    @pl.when(used)
    def _():
        x = xs_ref[...].astype(MXU_DTYPE)
        g = _dot(x, wg_s[...])
        u = _dot(x, wu_s[...])
        h = g * (1.0 / (1.0 + jnp.exp(-g))) * u
        ys_ref[...] = _dot(h.astype(MXU_DTYPE), wd_s[...])

    @pl.when(jnp.logical_not(used))
    def _():
        ys_ref[...] = jnp.zeros_like(ys_ref)


def _experts(xs, block_e, n_used, wg, wu, wd, layer, tb):
    p_rows, d = xs.shape
    f = wg.shape[3]
    nb = p_rows // tb
    rmap = lambda i, be, nu: (jnp.minimum(i, nu[0] - 1), 0)
    wmap = lambda i, be, nu: (layer, be[i], 0, 0)
    grid_spec = pltpu.PrefetchScalarGridSpec(
        num_scalar_prefetch=2,
        grid=(nb,),
        in_specs=[pl.BlockSpec((tb, d), rmap), pl.BlockSpec((1, 1, d, f), wmap),
                  pl.BlockSpec((1, 1, d, f), wmap), pl.BlockSpec((1, 1, f, d), wmap)],
        out_specs=pl.BlockSpec((tb, d), lambda i, be, nu: (i, 0)),
        scratch_shapes=[pltpu.VMEM((d, f), MXU_DTYPE), pltpu.VMEM((d, f), MXU_DTYPE),
                        pltpu.VMEM((f, d), MXU_DTYPE)])
    return pl.pallas_call(
        _expert_kernel,
        grid_spec=grid_spec,
        out_shape=jax.ShapeDtypeStruct((p_rows, d), F32),
        name="moe_experts",
        compiler_params=_params("arbitrary"),
    )(block_e, n_used, xs, wg, wu, wd)


def _combine_kernel(dest_ref, dnext_ref, x_ref, rw_ref, g_ref, b_ref, ys_ref, o_ref, ob_ref, buf, sem, *,
                    tm, alpha):
    i = pl.program_id(0)
    slot = i % 2

    def row_copy(dref, r, s, sl):
        return pltpu.make_async_copy(ys_ref.at[pl.ds(dref[0, 0, s * tm + r], 1)], buf.at[sl, s, pl.ds(r, 1)],
                                     sem.at[sl])

    def issue(dref, sl):
        def body(r, c):
            row_copy(dref, r, 0, sl).start(priority=0)
            row_copy(dref, r, 1, sl).start(priority=1)
            return c
        lax.fori_loop(0, tm, body, 0, unroll=DMA_UNROLL)

    @pl.when(i == 0)
    def _():
        issue(dest_ref, 0)

    @pl.when(i + 1 < pl.num_programs(0))
    def _():
        issue(dnext_ref, 1 - slot)

    def wait(r, c):
        row_copy(dest_ref, r, 0, slot).wait()
        row_copy(dest_ref, r, 1, slot).wait()
        return c

    lax.fori_loop(0, tm, wait, 0, unroll=DMA_UNROLL)
    rw = rw_ref[...]
    m = rw[:, 0:1] * buf[slot, 0] + rw[:, 1:2] * buf[slot, 1]
    out = _layer_norm(alpha * x_ref[...] + m, g_ref[...], b_ref[...])
    o_ref[...] = out
    ob_ref[...] = out.astype(ob_ref.dtype)


def _combine(x, ys, dest3, rw, g, b, alpha):
    n, d = x.shape
    n_steps = dest3.shape[0]
    tm = dest3.shape[2] // 2
    row = lambda i: (i, 0)
    fixed = lambda i: (0, 0)
    return pl.pallas_call(
        functools.partial(_combine_kernel, tm=tm, alpha=alpha),
        grid=(n_steps,),
        in_specs=[pl.BlockSpec((1, 1, 2 * tm), lambda i: (i, 0, 0), memory_space=pltpu.SMEM),
                  pl.BlockSpec((1, 1, 2 * tm), lambda i: (jnp.minimum(i + 1, n_steps - 1), 0, 0),
                               memory_space=pltpu.SMEM),
                  pl.BlockSpec((tm, d), row), pl.BlockSpec((tm, LANES), row),
                  pl.BlockSpec((1, d), fixed), pl.BlockSpec((1, d), fixed),
                  pl.BlockSpec(memory_space=pl.ANY)],
        out_specs=[pl.BlockSpec((tm, d), row), pl.BlockSpec((tm, d), row)],
        out_shape=[jax.ShapeDtypeStruct((n, d), F32), jax.ShapeDtypeStruct((n, d), MXU_DTYPE)],
        scratch_shapes=[pltpu.VMEM((2, 2, tm, d), F32), pltpu.SemaphoreType.DMA((2,))],
        name="moe_combine_norm",
        compiler_params=_params("arbitrary"),
    )(dest3, dest3, x, rw, g.reshape(1, d), b.reshape(1, d), ys)


def _moe(x, w_group, b_group, w_router, b_router, wg, wu, wd, layer, ln_g, ln_b, alpha):
    n, d = x.shape
    tb = MOE_BLOCK_ROWS
    n_groups = w_group.shape[1]
    n_experts = w_router.shape[1]
    per_group = n_experts // n_groups
    pad = LANES - n_groups - n_experts
    w_all = jnp.pad(jnp.concatenate([w_group, w_router], axis=1), ((0, 0), (0, pad)))
    b_all = jnp.pad(jnp.concatenate([b_group, b_router]), (0, pad)).reshape(1, LANES)
    wh = w_all.astype(MXU_DTYPE)
    wl = (w_all - wh.astype(F32)).astype(MXU_DTYPE)
    rt, rw, cnt = _router(x, wh, wl, b_all, n_groups, per_group)

    counts = cnt[0, :n_experts].astype(I32)
    padded = (counts + tb - 1) // tb * tb
    pend = jnp.cumsum(padded)
    poff = pend - padded
    expert = rt[0:2].astype(I32)
    offset = jnp.sum(jnp.where(expert[..., None] == jnp.arange(n_experts, dtype=I32), poff, 0), axis=-1)
    dest = offset + rt[2:4].astype(I32)
    tm = _pick_tile(n, DMA_ROWS, 16)
    dest3 = dest.reshape(2, n // tm, tm).transpose(1, 0, 2).reshape(n // tm, 1, 2 * tm)
    p_rows = (2 * n + n_experts * (tb - 1) + tb - 1) // tb * tb
    starts = jnp.arange(p_rows // tb, dtype=I32) * tb
    block_e = jnp.minimum(jnp.sum(pend[None, :] <= starts[:, None], axis=1), n_experts - 1).astype(I32)
    n_used = (pend[-1:] // tb).astype(I32)

    xs = _dispatch(x, dest3, pend.astype(I32), p_rows, tb)
    ys = _experts(xs, block_e, n_used, wg, wu, wd, layer, tb)
    return _combine(x, ys, dest3, rw, ln_g, ln_b, alpha)


def kernel(x_prompt, x_sample, cache_a_k, cache_a_v, cache_a_idx, cache_b_k, cache_b_v, a_w_in, a_w_o, b_w_q, b_w_kv, b_w_o, b_rel_bias, ln_mix_g, ln_mix_b, ln_ffn_g, ln_ffn_b, moe_w_group, moe_b_group, moe_w_router, moe_b_router, moe_w_gate, moe_w_up, moe_w_down):
    nb_p, t_p, d = x_prompt.shape
    nb_s, t_s, _ = x_sample.shape
    n_a, _, past, n_kv, head_dim = cache_a_k.shape
    idx_dim = cache_a_idx.shape[-1]
    depth = ln_mix_g.shape[0]
    n_heads = d // head_dim
    a_q = n_heads * head_dim
    a_kv = n_kv * head_dim
    n_idx_heads = (a_w_in.shape[-1] - a_q - 2 * a_kv - idx_dim) // (idx_dim + 1)
    a_qi = n_idx_heads * idx_dim
    window = LEFT_CHUNKS * CHUNK
    rel_clip = (b_rel_bias.shape[-1] - 1) // 2
    alpha = (2 * depth) ** 0.25
    attn_scale = head_dim ** -0.5
    np_rows = nb_p * t_p
    ns_rows = nb_s * t_s
    n = np_rows + ns_rows
    assert 2 * idx_dim == LANES and n_idx_heads % 2 == 0 and n_idx_heads <= LANES - idx_dim
    assert t_p % CHUNK == 0 and past % CHUNK == 0 and t_s <= CHUNK and window % t_s == 0
    assert cache_b_k.shape[1] == window

    cfg = dict(n_kv=n_kv, group=n_heads // n_kv, n_idx_heads=n_idx_heads, idx_dim=idx_dim,
               head_dim=head_dim, attn_scale=attn_scale,
               idx_scale=idx_dim ** -0.5 * n_idx_heads ** -0.5,
               slopes=tuple(2.0 ** (-8.0 * (h + 1) / n_heads) for h in range(n_heads)))

    tq_a = _pick_tile(t_p, DSA_TILE, LANES)
    tq_b = _pick_tile(t_p, BAND_TILE, CHUNK)
    l_s = past + t_s
    lp_s = -(-l_s // LANES) * LANES

    x = jnp.concatenate([x_prompt.reshape(np_rows, d), x_sample.reshape(ns_rows, d)], axis=0)
    xb = x
    new_k, new_v, new_i = [], [], []
    kb = vb = None
    for layer in range(depth):
        if layer < n_a:
            w = a_w_in[layer]
            o3 = a_q + 2 * a_kv
            w_kiwi = jnp.pad(w[:, o3 + a_qi:], ((0, 0), (0, LANES - idx_dim - n_idx_heads)))
            q = _mm(xb, w[:, :a_q].astype(MXU_DTYPE), MXU_DTYPE)
            k_new, k_mx = _mm_dual(xb, w[:, a_q:a_q + a_kv].astype(MXU_DTYPE))
            v_new, v_mx = _mm_dual(xb, w[:, a_q + a_kv:o3].astype(MXU_DTYPE))
            qi = _mm(xb, w[:, o3:o3 + a_qi].astype(MXU_DTYPE), MXU_DTYPE)
            kiwi = _mm(xb, w_kiwi.astype(MXU_DTYPE), F32)
            ki_new = kiwi[:, :idx_dim]
            ki_mx = ki_new.astype(MXU_DTYPE)
            new_k.append(k_new)
            new_v.append(v_new)
            new_i.append(ki_new)

            def key_side(arr, cache, width):
                kp = arr[:np_rows].reshape(nb_p, t_p, width)
                ks = jnp.concatenate([cache.astype(MXU_DTYPE).reshape(nb_s, past, width),
                                      arr[np_rows:].reshape(nb_s, t_s, width)], axis=1)
                return kp, jnp.pad(ks, ((0, 0), (0, lp_s - l_s), (0, 0)))

            k_p, k_s = key_side(k_mx, cache_a_k[layer], a_kv)
            v_p, v_s = key_side(v_mx, cache_a_v[layer], a_kv)
            i_p, i_s = key_side(ki_mx, cache_a_idx[layer], idx_dim)
            zpad = ((0, 0), (0, 0), (0, LANES - idx_dim))
            zpad_front = ((0, 0), (0, 0), (LANES - idx_dim, 0))
            keys_p = (k_p, v_p, jnp.pad(i_p, zpad), jnp.pad(i_p, zpad_front))
            keys_s = (k_s, v_s, jnp.pad(i_s, zpad), jnp.pad(i_s, zpad_front))
            o = jnp.zeros((n, a_q), MXU_DTYPE)
            for u in range(t_p // tq_a):
                o = _dsa(q, qi, kiwi, *keys_p, o, row0=0, rows=t_p, tq=tq_a, tile=u, lk=(u + 1) * tq_a,
                         l_valid=t_p, q_pos0=u * tq_a, k_top=min(TOPK_MAX, t_p // 4), cfg=cfg)
            o = _dsa(q, qi, kiwi, *keys_s, o, row0=np_rows, rows=t_s, tq=t_s, tile=0, lk=lp_s,
                     l_valid=l_s, q_pos0=past, k_top=min(TOPK_MAX, l_s // 4), cfg=cfg)
            w_o = a_w_o[layer]
        else:
            jb = layer - n_a
            q = _mm(xb, b_w_q[jb].astype(MXU_DTYPE), MXU_DTYPE)
            table = b_rel_bias[jb]
            o = jnp.zeros((n, d), MXU_DTYPE)
            o = _band(q, kb[0], vb[0], _band_bias(table, tq_b, window, rel_clip), o, row0=0, rows=t_p,
                      tq=tq_b, k_pos0=-window, head_dim=head_dim, attn_scale=attn_scale)
            o = _band(q, kb[1], vb[1], _band_bias(table, t_s, window, rel_clip), o, row0=np_rows, rows=t_s,
                      tq=t_s, k_pos0=past - window, head_dim=head_dim, attn_scale=attn_scale)
            w_o = b_w_o[jb]
        x, xb = _mm_ln(o, w_o.astype(MXU_DTYPE), x, ln_mix_g[layer], ln_mix_b[layer], alpha)
        x, xb = _moe(x, moe_w_group[layer], moe_b_group[layer], moe_w_router[layer], moe_b_router[layer],
                     moe_w_gate, moe_w_up, moe_w_down, layer, ln_ffn_g[layer], ln_ffn_b[layer], alpha)
        if layer == n_a - 1:
            kb_new, kb_mx = _mm_dual(xb, b_w_kv[:, :d].astype(MXU_DTYPE))
            vb_new, vb_mx = _mm_dual(xb, b_w_kv[:, d:].astype(MXU_DTYPE))

            def band_ctx(arr, cache):
                ctx_p = jnp.pad(arr[:np_rows].reshape(nb_p, t_p, d), ((0, 0), (window, 0), (0, 0)))
                ctx_s = jnp.concatenate([cache.astype(MXU_DTYPE).reshape(nb_s, window, d),
                                         arr[np_rows:].reshape(nb_s, t_s, d)], axis=1)
                return ctx_p, ctx_s

            kb = band_ctx(kb_mx, cache_b_k)
            vb = band_ctx(vb_mx, cache_b_v)

    keep = min(window, t_p)

    def split(arrs, shape_tail):
        st = jnp.stack(arrs)
        return (st[:, :np_rows].reshape((n_a, nb_p, t_p) + shape_tail),
                st[:, np_rows:].reshape((n_a, nb_s, t_s) + shape_tail))

    a_k_p, a_k_s = split(new_k, (n_kv, head_dim))
    a_v_p, a_v_s = split(new_v, (n_kv, head_dim))
    a_i_p, a_i_s = split(new_i, (idx_dim,))
    heads_b = (n_heads, head_dim)
    b_k_p = kb_new[:np_rows].reshape((nb_p, t_p) + heads_b)[:, t_p - keep:]
    b_v_p = vb_new[:np_rows].reshape((nb_p, t_p) + heads_b)[:, t_p - keep:]
    b_k_s = kb_new[np_rows:].reshape((nb_s, t_s) + heads_b)
    b_v_s = vb_new[np_rows:].reshape((nb_s, t_s) + heads_b)
    return (x[:np_rows].reshape(nb_p, t_p, d), x[np_rows:].reshape(nb_s, t_s, d),
            a_k_p, a_v_p, a_i_p, b_k_p, b_v_p, a_k_s, a_v_s, a_i_s, b_k_s, b_v_s)
```

```python
import functools
import math

import jax
import jax.numpy as jnp
from jax import lax
from jax.experimental import pallas as pl
from jax.experimental.pallas import tpu as pltpu

CHUNK = 64
TOPK_MAX = 256
LEFT_CHUNKS = 8
LN_EPS = 1e-5

LANES = 128
VMEM_LIMIT_BYTES = 56 * 1024 * 1024
MXU_DTYPE = jnp.bfloat16

MM_ROWS = 768
LN_ROWS = 384
DMA_ROWS = 256
COMBINE_PARTS = 4
MOE_BLOCK_ROWS = 512
ROUTE_FIELDS = 8
DSA_TILE = 256
SEARCH_ROWS = 128
BAND_TILE = 256
BAND_HEADS = 8

F32 = jnp.float32
I32 = jnp.int32
NEG_INF = float("-inf")
INT_MIN = -(2 ** 31)
LOG2E = math.log2(math.e)


def _pick_tile(n, cap, mult):
    best = None
    for t in range(mult, min(n, cap) + 1, mult):
        if n % t == 0:
            best = t
    if best is None:
        raise ValueError(f"no tile for n={n} cap={cap} mult={mult}")
    return best


def _params(*sem):
    return pltpu.CompilerParams(dimension_semantics=sem, vmem_limit_bytes=VMEM_LIMIT_BYTES)


def _dot(a, b):
    return jnp.dot(a, b, preferred_element_type=F32)


def _dot_nt(a, b):
    return lax.dot_general(a, b, (((1,), (1,)), ((), ())), preferred_element_type=F32)


def _layer_norm(z, g, b):
    mu = jnp.mean(z, axis=-1, keepdims=True)
    zc = z - mu
    var = jnp.mean(zc * zc, axis=-1, keepdims=True)
    return zc * lax.rsqrt(var + LN_EPS) * g + b


def _mm_kernel(x_ref, w_ref, o_ref):
    o_ref[...] = _dot(x_ref[...].astype(MXU_DTYPE), w_ref[...]).astype(o_ref.dtype)


def _mm(x, w, out_dtype):
    n, k = x.shape
    m = w.shape[1]
    tm = _pick_tile(n, MM_ROWS, 16)
    tn = _pick_tile(m, 2048, LANES)
    return pl.pallas_call(
        _mm_kernel,
        grid=(m // tn, n // tm),
        in_specs=[pl.BlockSpec((tm, k), lambda j, i: (i, 0)),
                  pl.BlockSpec((k, tn), lambda j, i: (0, j))],
        out_specs=pl.BlockSpec((tm, tn), lambda j, i: (i, j)),
        out_shape=jax.ShapeDtypeStruct((n, m), out_dtype),
        name="projection",
        compiler_params=_params("arbitrary", "arbitrary"),
    )(x, w)


def _mm_dual_kernel(x_ref, w_ref, o_ref, ob_ref):
    y = _dot(x_ref[...].astype(MXU_DTYPE), w_ref[...])
    o_ref[...] = y
    ob_ref[...] = y.astype(ob_ref.dtype)


def _mm_dual(x, w):
    n, k = x.shape
    m = w.shape[1]
    tm = _pick_tile(n, MM_ROWS, 16)
    return pl.pallas_call(
        _mm_dual_kernel,
        grid=(n // tm,),
        in_specs=[pl.BlockSpec((tm, k), lambda i: (i, 0)),
                  pl.BlockSpec((k, m), lambda i: (0, 0))],
        out_specs=[pl.BlockSpec((tm, m), lambda i: (i, 0)), pl.BlockSpec((tm, m), lambda i: (i, 0))],
        out_shape=[jax.ShapeDtypeStruct((n, m), F32), jax.ShapeDtypeStruct((n, m), MXU_DTYPE)],
        name="projection_dual",
        compiler_params=_params("arbitrary"),
    )(x, w)


def _mm_ln_kernel(a_ref, w_ref, x_ref, g_ref, b_ref, o_ref, ob_ref, *, alpha):
    y = _dot(a_ref[...], w_ref[...])
    out = _layer_norm(alpha * x_ref[...] + y, g_ref[...], b_ref[...])
    o_ref[...] = out
    ob_ref[...] = out.astype(ob_ref.dtype)


def _mm_ln(a, w, x, g, b, alpha):
    n, k = a.shape
    d = w.shape[1]
    tm = _pick_tile(n, LN_ROWS, 16)
    row = lambda i: (i, 0)
    fixed = lambda i: (0, 0)
    return pl.pallas_call(
        functools.partial(_mm_ln_kernel, alpha=alpha),
        grid=(n // tm,),
        in_specs=[pl.BlockSpec((tm, k), row), pl.BlockSpec((k, d), fixed),
                  pl.BlockSpec((tm, d), row), pl.BlockSpec((1, d), fixed),
                  pl.BlockSpec((1, d), fixed)],
        out_specs=[pl.BlockSpec((tm, d), row), pl.BlockSpec((tm, d), row)],
        out_shape=[jax.ShapeDtypeStruct((n, d), F32), jax.ShapeDtypeStruct((n, d), MXU_DTYPE)],
        name="out_projection_norm",
        compiler_params=_params("arbitrary"),
    )(a, w, x, g.reshape(1, d), b.reshape(1, d))


def _row_total(acc, ones_blk):
    return _dot(acc.astype(MXU_DTYPE), ones_blk)


def _dsa_kernel(q_ref, qi_ref, wq_ref, k_ref, v_ref, kia_ref, kib_ref, prev_ref, o_ref,
                key_ref, msk_ref, *, tq, lk, l_valid, q_pos0, k_top, n_kv, group,
                n_idx_heads, idx_dim, head_dim, slopes, idx_scale, attn_scale):
    del prev_ref
    nslab = lk // LANES
    q_pos = q_pos0 + lax.broadcasted_iota(I32, (tq, lk), 0)
    k_pos = lax.broadcasted_iota(I32, (tq, lk), 1)
    admissible = ((k_pos // CHUNK) <= (q_pos // CHUNK)) & (k_pos < l_valid)

    wq = wq_ref[...]
    score = jnp.zeros((tq, lk), F32)
    for p in range(n_idx_heads // 2):
        qi2 = qi_ref[:, p * 2 * idx_dim:(p + 1) * 2 * idx_dim]
        s_a = _dot_nt(qi2, kia_ref[0])
        s_b = _dot_nt(qi2, kib_ref[0])
        w_a = wq[:, idx_dim + 2 * p:idx_dim + 2 * p + 1]
        w_b = wq[:, idx_dim + 2 * p + 1:idx_dim + 2 * p + 2]
        score = score + w_a * jnp.maximum(s_a, 0.0) + w_b * jnp.maximum(s_b, 0.0)
    score = score * idx_scale + 0.0

    bits = pltpu.bitcast(score, I32)
    key = bits ^ ((bits >> 31) & 0x7FFFFFFF)
    key_ref[...] = jnp.where(admissible, key, INT_MIN)

    ones_blk = jnp.ones((LANES, LANES), MXU_DTYPE)
    rows = min(tq, SEARCH_ROWS)
    groups = [slice(c * rows, (c + 1) * rows) for c in range(tq // rows)]

    def count(rs, pred):
        acc = jnp.zeros((rows, LANES), F32)
        for c in range(nslab):
            acc = acc + jnp.where(pred(key_ref[rs, c * LANES:(c + 1) * LANES]), 1.0, 0.0)
        return jnp.sum(acc, axis=-1, keepdims=True)

    def lanes(col):
        return jnp.broadcast_to(col, (rows, LANES))

    kf = float(k_top)
    thr0 = tuple(jnp.where(count(rs, lambda x: x >= 0) >= kf, 0, INT_MIN).astype(I32) for rs in groups)

    def bit_step(i, thrs):
        bit = lax.shift_left(jnp.int32(1), jnp.asarray(30 - i, dtype=I32))
        out = []
        for rs, t in zip(groups, thrs):
            cand = t | bit
            cand_l = lanes(cand)
            out.append(jnp.where(count(rs, lambda x, c=cand_l: x >= c) >= kf, cand, t))
        return tuple(out)

    thrs = lax.fori_loop(0, 31, bit_step, thr0)

    for rs, thr in zip(groups, thrs):
        thr = lanes(jnp.maximum(thr, INT_MIN + 1))
        n_gt = count(rs, lambda x, thr=thr: x > thr)
        n_eq = count(rs, lambda x, thr=thr: x == thr)
        for c in range(nslab):
            sl = slice(c * LANES, (c + 1) * LANES)
            msk_ref[rs, sl] = jnp.where(key_ref[rs, sl] >= thr, 0.0, NEG_INF)

        has_ties = jnp.max(jnp.where(n_gt + n_eq > kf, 1.0, 0.0)) > 0.0

        @pl.when(has_ties)
        def _(rs=rs, thr=thr, n_gt=n_gt):
            need = kf - n_gt
            upper = (lax.broadcasted_iota(I32, (LANES, LANES), 0)
                     < lax.broadcasted_iota(I32, (LANES, LANES), 1)).astype(MXU_DTYPE)
            before = jnp.zeros((rows, LANES), F32)
            for c in range(nslab):
                sl = slice(c * LANES, (c + 1) * LANES)
                ks = key_ref[rs, sl]
                eq = jnp.where(ks == thr, 1.0, 0.0)
                rank = before + _dot(eq.astype(MXU_DTYPE), upper)
                keep = (ks > thr) | ((ks == thr) & (rank < need))
                msk_ref[rs, sl] = jnp.where(keep, 0.0, NEG_INF)
                before = before + _row_total(eq, ones_blk)

    dist = jnp.abs(q_pos - k_pos).astype(F32)
    for g in range(n_kv):
        cols = slice(g * head_dim, (g + 1) * head_dim)
        kg = k_ref[0, :, cols]
        vg = v_ref[0, :, cols]
        q4 = jnp.concatenate(
            [q_ref[:, (g * group + r) * head_dim:(g * group + r + 1) * head_dim] for r in range(group)],
            axis=0)
        logits = _dot_nt(q4, kg)
        for r in range(group):
            h = g * group + r
            z = logits[r * tq:(r + 1) * tq] * (attn_scale * LOG2E) - (slopes[h] * LOG2E) * dist + msk_ref[...]
            m = jnp.max(z, axis=-1, keepdims=True)
            p = jnp.exp2(z - m)
            denom = jnp.sum(p, axis=-1, keepdims=True)
            o = _dot(p.astype(MXU_DTYPE), vg) / denom
            o_ref[:, h * head_dim:(h + 1) * head_dim] = o.astype(o_ref.dtype)


def _dsa(q, qi, kiwi, k, v, kia, kib, prev, *, row0, rows, tq, tile, lk, l_valid, q_pos0, k_top, cfg):
    nb, _, a_kv = k.shape
    n_rows, a_q = q.shape
    n_q = rows // tq
    blk0 = row0 // tq
    qmap = lambda b: (blk0 + b * n_q + tile, 0)
    kmap = lambda b: (b, 0, 0)
    kern = functools.partial(
        _dsa_kernel, tq=tq, lk=lk, l_valid=l_valid, q_pos0=q_pos0, k_top=k_top,
        n_kv=cfg["n_kv"], group=cfg["group"], n_idx_heads=cfg["n_idx_heads"],
        idx_dim=cfg["idx_dim"], head_dim=cfg["head_dim"], slopes=cfg["slopes"],
        idx_scale=cfg["idx_scale"], attn_scale=cfg["attn_scale"])
    return pl.pallas_call(
        kern,
        grid=(nb,),
        in_specs=[pl.BlockSpec((tq, a_q), qmap),
                  pl.BlockSpec((tq, qi.shape[1]), qmap),
                  pl.BlockSpec((tq, LANES), qmap),
                  pl.BlockSpec((1, lk, a_kv), kmap),
                  pl.BlockSpec((1, lk, a_kv), kmap),
                  pl.BlockSpec((1, lk, LANES), kmap),
                  pl.BlockSpec((1, lk, LANES), kmap),
                  pl.BlockSpec(memory_space=pl.ANY)],
        out_specs=pl.BlockSpec((tq, a_q), qmap),
        out_shape=jax.ShapeDtypeStruct((n_rows, a_q), MXU_DTYPE),
        scratch_shapes=[pltpu.VMEM((tq, lk), I32), pltpu.VMEM((tq, lk), F32)],
        input_output_aliases={7: 0},
        name="dsa_attention",
        compiler_params=_params("arbitrary"),
    )(q, qi, kiwi, k, v, kia, kib, prev)


def _band_kernel(q_ref, k_ref, v_ref, bias_ref, prev_ref, o_ref, *, tq, win, k_pos0, heads, head_dim,
                 attn_scale):
    del prev_ref
    j = pl.program_id(2)
    start = pl.multiple_of(j * tq, tq)
    k_pos = k_pos0 + j * tq + lax.broadcasted_iota(I32, (tq, win), 1)
    before_start = jnp.where(k_pos >= 0, 0.0, NEG_INF)
    for r in range(heads):
        cols = slice(r * head_dim, (r + 1) * head_dim)
        kw = k_ref[0, pl.ds(start, win), cols]
        vw = v_ref[0, pl.ds(start, win), cols]
        z = _dot_nt(q_ref[:, cols], kw) * (attn_scale * LOG2E) + bias_ref[r] + before_start
        m = jnp.max(z, axis=-1, keepdims=True)
        p = jnp.exp2(z - m)
        denom = jnp.sum(p, axis=-1, keepdims=True)
        o = _dot(p.astype(MXU_DTYPE), vw) / denom
        o_ref[:, cols] = o.astype(o_ref.dtype)


def _band(q, kpad, vpad, bias, prev, *, row0, rows, tq, k_pos0, head_dim, attn_scale):
    nb, lk, d = kpad.shape
    n_rows = q.shape[0]
    n_heads, _, win = bias.shape
    hg = BAND_HEADS if n_heads % BAND_HEADS == 0 else 1
    wcol = hg * head_dim
    n_q = rows // tq
    blk0 = row0 // tq
    qmap = lambda b, g, j: (blk0 + b * n_q + j, g)
    kmap = lambda b, g, j: (b, 0, g)
    kern = functools.partial(_band_kernel, tq=tq, win=win, k_pos0=k_pos0, heads=hg, head_dim=head_dim,
                             attn_scale=attn_scale)
    return pl.pallas_call(
        kern,
        grid=(nb, n_heads // hg, n_q),
        in_specs=[pl.BlockSpec((tq, wcol), qmap),
                  pl.BlockSpec((1, lk, wcol), kmap),
                  pl.BlockSpec((1, lk, wcol), kmap),
                  pl.BlockSpec((hg, tq, win), lambda b, g, j: (g, 0, 0)),
                  pl.BlockSpec(memory_space=pl.ANY)],
        out_specs=pl.BlockSpec((tq, wcol), qmap),
        out_shape=jax.ShapeDtypeStruct((n_rows, d), MXU_DTYPE),
        input_output_aliases={4: 0},
        name="band_attention",
        compiler_params=_params("arbitrary", "arbitrary", "arbitrary"),
    )(q, kpad, vpad, bias, prev)


def _band_bias(table, tq, window, rel_clip):
    n_heads = table.shape[0]
    w = window + tq
    ext = w + tq - 1
    x = jnp.arange(ext, dtype=I32)
    rel = jnp.clip(window + tq - 1 - x, -rel_clip, rel_clip) + rel_clip
    e = jnp.pad(table[:, rel].astype(F32) * LOG2E, ((0, 0), (0, 1)))
    skew = jnp.broadcast_to(e[:, None, :], (n_heads, tq, ext + 1)).reshape(n_heads, tq * (ext + 1))
    skew = skew[:, :tq * ext].reshape(n_heads, tq, ext)[:, :, tq - 1:tq - 1 + w]
    qi = jnp.arange(tq, dtype=I32)[:, None]
    kj = jnp.arange(w, dtype=I32)[None, :]
    qc = qi // CHUNK
    kc = kj // CHUNK - window // CHUNK
    band = (kc <= qc) & (kc >= qc - LEFT_CHUNKS)
    return jnp.where(band[None], skew, NEG_INF)


def _router_kernel(x_ref, wh_ref, wl_ref, b_ref, rt_ref, rw_ref, cnt_ref, run_ref, *, tm, n_groups,
                   per_group):
    i = pl.program_id(0)

    @pl.when(i == 0)
    def _():
        run_ref[...] = jnp.zeros_like(run_ref)

    x = x_ref[...]
    xh = x.astype(MXU_DTYPE)
    xl = (x - xh.astype(F32)).astype(MXU_DTYPE)
    logits = _dot(xh, wh_ref[...]) + _dot(xl, wh_ref[...]) + _dot(xh, wl_ref[...]) + b_ref[...]

    lane = lax.broadcasted_iota(I32, (tm, LANES), 1)
    gl = jnp.where(lane < n_groups, logits, NEG_INF)
    gmax = jnp.max(gl, axis=-1, keepdims=True)
    gsel = jnp.min(jnp.where(gl == gmax, lane, LANES), axis=-1, keepdims=True)
    p_group = 1.0 / jnp.sum(jnp.exp(gl - gmax), axis=-1, keepdims=True)

    lo = n_groups + gsel * per_group
    el = jnp.where((lane >= lo) & (lane < lo + per_group), logits, NEG_INF)
    v1 = jnp.max(el, axis=-1, keepdims=True)
    i1 = jnp.min(jnp.where(el == v1, lane, LANES), axis=-1, keepdims=True)
    el2 = jnp.where(lane == i1, NEG_INF, el)
    v2 = jnp.max(el2, axis=-1, keepdims=True)
    i2 = jnp.min(jnp.where(el2 == v2, lane, LANES), axis=-1, keepdims=True)
    t = jnp.exp(v2 - v1)
    w1 = p_group / (1.0 + t)
    w2 = p_group * t / (1.0 + t)
    e1 = i1 - n_groups
    e2 = i2 - n_groups

    oh1 = jnp.where(lane == e1, 1.0, 0.0)
    oh2 = jnp.where(lane == e2, 1.0, 0.0)
    lower = (lax.broadcasted_iota(I32, (tm, tm), 1)
             < lax.broadcasted_iota(I32, (tm, tm), 0)).astype(MXU_DTYPE)
    pre1 = _dot(lower, oh1.astype(MXU_DTYPE))
    pre2 = _dot(lower, oh2.astype(MXU_DTYPE))
    c1 = jnp.sum(oh1, axis=0, keepdims=True)
    c2 = jnp.sum(oh2, axis=0, keepdims=True)
    run = run_ref[...]
    r1 = jnp.sum(oh1 * (pre1 + run), axis=-1, keepdims=True)
    r2 = jnp.sum(oh2 * (pre2 + run + c1), axis=-1, keepdims=True)
    run = run + c1 + c2
    run_ref[...] = run
    cnt_ref[...] = run

    fields = jnp.where(lane == 0, e1.astype(F32), jnp.where(lane == 1, e2.astype(F32), jnp.where(
        lane == 2, r1, jnp.where(lane == 3, r2, 0.0))))
    rt_ref[...] = fields.T[0:ROUTE_FIELDS]
    rw_ref[...] = jnp.where(lane == 0, w1, jnp.where(lane == 1, w2, 0.0))


def _router(x, wh, wl, bias, n_groups, per_group):
    n, d = x.shape
    tm = _pick_tile(n, LN_ROWS, LANES)
    row = lambda i: (i, 0)
    fixed = lambda i: (0, 0)
    return pl.pallas_call(
        functools.partial(_router_kernel, tm=tm, n_groups=n_groups, per_group=per_group),
        grid=(n // tm,),
        in_specs=[pl.BlockSpec((tm, d), row), pl.BlockSpec((d, LANES), fixed),
                  pl.BlockSpec((d, LANES), fixed), pl.BlockSpec((1, LANES), fixed)],
        out_specs=[pl.BlockSpec((ROUTE_FIELDS, tm), lambda i: (0, i)), pl.BlockSpec((tm, LANES), row),
                   pl.BlockSpec((1, LANES), fixed)],
        out_shape=[jax.ShapeDtypeStruct((ROUTE_FIELDS, n), F32), jax.ShapeDtypeStruct((n, LANES), F32),
                   jax.ShapeDtypeStruct((1, LANES), F32)],
        scratch_shapes=[pltpu.VMEM((1, LANES), F32)],
        name="moe_router",
        compiler_params=_params("arbitrary"),
    )(x, wh, wl, bias)


def _dispatch_kernel(pend_ref, d0_ref, d1_ref, x_ref, xs_ref, zero_ref, sem, zsem, *, tm, tb, n_experts):
    @pl.when(pl.program_id(0) == 0)
    def _():
        zero_ref[...] = jnp.zeros_like(zero_ref)

        def block_copy(e):
            first = pl.multiple_of(jnp.maximum(pend_ref[e] - tb, 0), tb)
            return pltpu.make_async_copy(zero_ref, xs_ref.at[pl.ds(first, tb)], zsem)

        def nonempty(e):
            return pend_ref[e] > (pend_ref[e - 1] if e else 0)

        for e in range(n_experts):
            pl.when(nonempty(e))(lambda e=e: block_copy(e).start())
        for e in range(n_experts):
            pl.when(nonempty(e))(lambda e=e: block_copy(e).wait())

        def tail_copy(j):
            return pltpu.make_async_copy(zero_ref, xs_ref.at[pl.ds(pl.multiple_of(j * tb, tb), tb)], zsem)

        n_used = pend_ref[n_experts - 1] // tb
        n_blocks = xs_ref.shape[0] // tb
        lax.fori_loop(n_used, n_blocks, lambda j, c: (tail_copy(j).start(), c)[1], 0)
        lax.fori_loop(n_used, n_blocks, lambda j, c: (tail_copy(j).wait(), c)[1], 0)

    dest_refs = (d0_ref, d1_ref)

    def row_copy(r, s):
        return pltpu.make_async_copy(x_ref.at[pl.ds(r, 1)], xs_ref.at[pl.ds(dest_refs[s][0, 0, r], 1)], sem)

    for r in range(tm):
        row_copy(r, 0).start(priority=0)
        row_copy(r, 1).start(priority=1)
    for r in range(tm):
        row_copy(r, 0).wait()
        row_copy(r, 1).wait()


def _dispatch(x, dests, pend, p_rows, tb):
    n, d = x.shape
    tm = dests[0].shape[2]
    slot_spec = pl.BlockSpec((1, 1, tm), lambda i, pe: (i, 0, 0), memory_space=pltpu.SMEM)
    grid_spec = pltpu.PrefetchScalarGridSpec(
        num_scalar_prefetch=1,
        grid=(n // tm,),
        in_specs=[slot_spec, slot_spec, pl.BlockSpec((tm, d), lambda i, pe: (i, 0))],
        out_specs=pl.BlockSpec(memory_space=pl.ANY),
        scratch_shapes=[pltpu.VMEM((tb, d), x.dtype), pltpu.SemaphoreType.DMA, pltpu.SemaphoreType.DMA])
    return pl.pallas_call(
        functools.partial(_dispatch_kernel, tm=tm, tb=tb, n_experts=pend.shape[0]),
        grid_spec=grid_spec,
        out_shape=jax.ShapeDtypeStruct((p_rows, d), x.dtype),
        name="moe_dispatch",
        compiler_params=_params("arbitrary"),
    )(pend, dests[0], dests[1], x)


def _expert_kernel(be_ref, nu_ref, xs_ref, wg_ref, wu_ref, wd_ref, ys_ref, wg_s, wu_s, wd_s):
    i = pl.program_id(0)
    used = i < nu_ref[0]
    new_expert = jnp.logical_or(i == 0, be_ref[i] != be_ref[jnp.maximum(i - 1, 0)])

    @pl.when(jnp.logical_and(used, new_expert))
    def _():
        wg_s[...] = wg_ref[0, 0].astype(MXU_DTYPE)
        wu_s[...] = wu_ref[0, 0].astype(MXU_DTYPE)
        wd_s[...] = wd_ref[0, 0].astype(MXU_DTYPE)

    @pl.when(used)
    def _():
        x = xs_ref[...].astype(MXU_DTYPE)
        g = _dot(x, wg_s[...])
        u = _dot(x, wu_s[...])
        h = g * (1.0 / (1.0 + jnp.exp(-g))) * u
        ys_ref[...] = _dot(h.astype(MXU_DTYPE), wd_s[...])

    @pl.when(jnp.logical_not(used))
    def _():
        ys_ref[...] = jnp.zeros_like(ys_ref)


def _experts(xs, block_e, n_used, wg, wu, wd, layer, tb):
    p_rows, d = xs.shape
    f = wg.shape[3]
    nb = p_rows // tb
    rmap = lambda i, be, nu: (jnp.minimum(i, nu[0] - 1), 0)
    wmap = lambda i, be, nu: (layer, be[i], 0, 0)
    grid_spec = pltpu.PrefetchScalarGridSpec(
        num_scalar_prefetch=2,
        grid=(nb,),
        in_specs=[pl.BlockSpec((tb, d), rmap), pl.BlockSpec((1, 1, d, f), wmap),
                  pl.BlockSpec((1, 1, d, f), wmap), pl.BlockSpec((1, 1, f, d), wmap)],
        out_specs=pl.BlockSpec((tb, d), lambda i, be, nu: (i, 0)),
        scratch_shapes=[pltpu.VMEM((d, f), MXU_DTYPE), pltpu.VMEM((d, f), MXU_DTYPE),
                        pltpu.VMEM((f, d), MXU_DTYPE)])
    return pl.pallas_call(
        _expert_kernel,
        grid_spec=grid_spec,
        out_shape=jax.ShapeDtypeStruct((p_rows, d), F32),
        name="moe_experts",
        compiler_params=_params("arbitrary"),
    )(block_e, n_used, xs, wg, wu, wd)


def _combine_kernel(d0_ref, d1_ref, n0_ref, n1_ref, x_ref, rw_ref, g_ref, b_ref, ys_ref, o_ref, ob_ref, buf,
                    sem, *, tm, alpha):
    i = pl.program_id(0)
    last = pl.num_programs(0) - 1
    slot = i % 2
    other = 1 - slot
    here = (d0_ref, d1_ref)
    ahead = (n0_ref, n1_ref)

    def row_copy(drefs, r, s, sl):
        return pltpu.make_async_copy(ys_ref.at[pl.ds(drefs[s][0, 0, r], 1)], buf.at[sl, s, pl.ds(r, 1)],
                                     sem.at[sl])

    def issue(drefs, sl, rows):
        for r in rows:
            row_copy(drefs, r, 0, sl).start(priority=0)
            row_copy(drefs, r, 1, sl).start(priority=1)

    def wait_all(sl):
        for r in range(tm):
            row_copy(here, r, 0, sl).wait()
            row_copy(here, r, 1, sl).wait()

    @pl.when(i == 0)
    def _():
        issue(here, 0, range(tm))

    wait_all(slot)

    rows_per_part = tm // COMBINE_PARTS
    for part in range(COMBINE_PARTS):
        rs = slice(part * rows_per_part, (part + 1) * rows_per_part)
        issue(ahead, other, range(rs.start, rs.stop))
        rw = rw_ref[rs]
        m = rw[:, 0:1] * buf[slot, 0, rs] + rw[:, 1:2] * buf[slot, 1, rs]
        out = _layer_norm(alpha * x_ref[rs] + m, g_ref[...], b_ref[...])
        o_ref[rs] = out
        ob_ref[rs] = out.astype(ob_ref.dtype)

    @pl.when(i == last)
    def _():
        wait_all(other)


def _combine(x, ys, dests, rw, g, b, alpha):
    n, d = x.shape
    n_steps, _, tm = dests[0].shape
    row = lambda i: (i, 0)
    fixed = lambda i: (0, 0)
    here = pl.BlockSpec((1, 1, tm), lambda i: (i, 0, 0), memory_space=pltpu.SMEM)
    ahead = pl.BlockSpec((1, 1, tm), lambda i: (jnp.minimum(i + 1, n_steps - 1), 0, 0), memory_space=pltpu.SMEM)
    return pl.pallas_call(
        functools.partial(_combine_kernel, tm=tm, alpha=alpha),
        grid=(n_steps,),
        in_specs=[here, here, ahead, ahead,
                  pl.BlockSpec((tm, d), row), pl.BlockSpec((tm, LANES), row),
                  pl.BlockSpec((1, d), fixed), pl.BlockSpec((1, d), fixed),
                  pl.BlockSpec(memory_space=pl.ANY)],
        out_specs=[pl.BlockSpec((tm, d), row), pl.BlockSpec((tm, d), row)],
        out_shape=[jax.ShapeDtypeStruct((n, d), F32), jax.ShapeDtypeStruct((n, d), MXU_DTYPE)],
        scratch_shapes=[pltpu.VMEM((2, 2, tm, d), F32), pltpu.SemaphoreType.DMA((2,))],
        name="moe_combine_norm",
        compiler_params=_params("arbitrary"),
    )(dests[0], dests[1], dests[0], dests[1], x, rw, g.reshape(1, d), b.reshape(1, d), ys)


def _moe(x, w_group, b_group, w_router, b_router, wg, wu, wd, layer, ln_g, ln_b, alpha):
    n, d = x.shape
    tb = MOE_BLOCK_ROWS
    n_groups = w_group.shape[1]
    n_experts = w_router.shape[1]
    per_group = n_experts // n_groups
    pad = LANES - n_groups - n_experts
    w_all = jnp.pad(jnp.concatenate([w_group, w_router], axis=1), ((0, 0), (0, pad)))
    b_all = jnp.pad(jnp.concatenate([b_group, b_router]), (0, pad)).reshape(1, LANES)
    wh = w_all.astype(MXU_DTYPE)
    wl = (w_all - wh.astype(F32)).astype(MXU_DTYPE)
    rt, rw, cnt = _router(x, wh, wl, b_all, n_groups, per_group)

    counts = cnt[0, :n_experts].astype(I32)
    padded = (counts + tb - 1) // tb * tb
    pend = jnp.cumsum(padded)
    poff = pend - padded
    expert = rt[0:2].astype(I32)
    offset = jnp.sum(jnp.where(expert[..., None] == jnp.arange(n_experts, dtype=I32), poff, 0), axis=-1)
    dest = offset + rt[2:4].astype(I32)
    tm = _pick_tile(n, DMA_ROWS, 16 * COMBINE_PARTS)
    dests = (dest[0].reshape(n // tm, 1, tm), dest[1].reshape(n // tm, 1, tm))
    p_rows = (2 * n + n_experts * (tb - 1) + tb - 1) // tb * tb
    starts = jnp.arange(p_rows // tb, dtype=I32) * tb
    block_e = jnp.minimum(jnp.sum(pend[None, :] <= starts[:, None], axis=1), n_experts - 1).astype(I32)
    n_used = (pend[-1:] // tb).astype(I32)

    xs = _dispatch(x, dests, pend.astype(I32), p_rows, tb)
    ys = _experts(xs, block_e, n_used, wg, wu, wd, layer, tb)
    return _combine(x, ys, dests, rw, ln_g, ln_b, alpha)


def kernel(x_prompt, x_sample, cache_a_k, cache_a_v, cache_a_idx, cache_b_k, cache_b_v, a_w_in, a_w_o, b_w_q, b_w_kv, b_w_o, b_rel_bias, ln_mix_g, ln_mix_b, ln_ffn_g, ln_ffn_b, moe_w_group, moe_b_group, moe_w_router, moe_b_router, moe_w_gate, moe_w_up, moe_w_down):
    nb_p, t_p, d = x_prompt.shape
    nb_s, t_s, _ = x_sample.shape
    n_a, _, past, n_kv, head_dim = cache_a_k.shape
    idx_dim = cache_a_idx.shape[-1]
    depth = ln_mix_g.shape[0]
    n_heads = d // head_dim
    a_q = n_heads * head_dim
    a_kv = n_kv * head_dim
    n_idx_heads = (a_w_in.shape[-1] - a_q - 2 * a_kv - idx_dim) // (idx_dim + 1)
    a_qi = n_idx_heads * idx_dim
    window = LEFT_CHUNKS * CHUNK
    rel_clip = (b_rel_bias.shape[-1] - 1) // 2
    alpha = (2 * depth) ** 0.25
    attn_scale = head_dim ** -0.5
    np_rows = nb_p * t_p
    ns_rows = nb_s * t_s
    n = np_rows + ns_rows
    assert 2 * idx_dim == LANES and n_idx_heads % 2 == 0 and n_idx_heads <= LANES - idx_dim
    assert t_p % CHUNK == 0 and past % CHUNK == 0 and t_s <= CHUNK and window % t_s == 0
    assert cache_b_k.shape[1] == window

    cfg = dict(n_kv=n_kv, group=n_heads // n_kv, n_idx_heads=n_idx_heads, idx_dim=idx_dim,
               head_dim=head_dim, attn_scale=attn_scale,
               idx_scale=idx_dim ** -0.5 * n_idx_heads ** -0.5,
               slopes=tuple(2.0 ** (-8.0 * (h + 1) / n_heads) for h in range(n_heads)))

    tq_a = _pick_tile(t_p, DSA_TILE, LANES)
    tq_b = _pick_tile(t_p, BAND_TILE, CHUNK)
    l_s = past + t_s
    lp_s = -(-l_s // LANES) * LANES

    x = jnp.concatenate([x_prompt.reshape(np_rows, d), x_sample.reshape(ns_rows, d)], axis=0)
    xb = x
    new_k, new_v, new_i = [], [], []
    kb = vb = None
    for layer in range(depth):
        if layer < n_a:
            w = a_w_in[layer]
            o3 = a_q + 2 * a_kv
            w_kiwi = jnp.pad(w[:, o3 + a_qi:], ((0, 0), (0, LANES - idx_dim - n_idx_heads)))
            q = _mm(xb, w[:, :a_q].astype(MXU_DTYPE), MXU_DTYPE)
            k_new, k_mx = _mm_dual(xb, w[:, a_q:a_q + a_kv].astype(MXU_DTYPE))
            v_new, v_mx = _mm_dual(xb, w[:, a_q + a_kv:o3].astype(MXU_DTYPE))
            qi = _mm(xb, w[:, o3:o3 + a_qi].astype(MXU_DTYPE), MXU_DTYPE)
            kiwi = _mm(xb, w_kiwi.astype(MXU_DTYPE), F32)
            ki_new = kiwi[:, :idx_dim]
            ki_mx = ki_new.astype(MXU_DTYPE)
            new_k.append(k_new)
            new_v.append(v_new)
            new_i.append(ki_new)

            def key_side(arr, cache, width):
                kp = arr[:np_rows].reshape(nb_p, t_p, width)
                ks = jnp.concatenate([cache.astype(MXU_DTYPE).reshape(nb_s, past, width),
                                      arr[np_rows:].reshape(nb_s, t_s, width)], axis=1)
                return kp, jnp.pad(ks, ((0, 0), (0, lp_s - l_s), (0, 0)))

            k_p, k_s = key_side(k_mx, cache_a_k[layer], a_kv)
            v_p, v_s = key_side(v_mx, cache_a_v[layer], a_kv)
            i_p, i_s = key_side(ki_mx, cache_a_idx[layer], idx_dim)
            zpad = ((0, 0), (0, 0), (0, LANES - idx_dim))
            zpad_front = ((0, 0), (0, 0), (LANES - idx_dim, 0))
            keys_p = (k_p, v_p, jnp.pad(i_p, zpad), jnp.pad(i_p, zpad_front))
            keys_s = (k_s, v_s, jnp.pad(i_s, zpad), jnp.pad(i_s, zpad_front))
            o = jnp.zeros((n, a_q), MXU_DTYPE)
            for u in range(t_p // tq_a):
                o = _dsa(q, qi, kiwi, *keys_p, o, row0=0, rows=t_p, tq=tq_a, tile=u, lk=(u + 1) * tq_a,
                         l_valid=t_p, q_pos0=u * tq_a, k_top=min(TOPK_MAX, t_p // 4), cfg=cfg)
            o = _dsa(q, qi, kiwi, *keys_s, o, row0=np_rows, rows=t_s, tq=t_s, tile=0, lk=lp_s,
                     l_valid=l_s, q_pos0=past, k_top=min(TOPK_MAX, l_s // 4), cfg=cfg)
            w_o = a_w_o[layer]
        else:
            jb = layer - n_a
            q = _mm(xb, b_w_q[jb].astype(MXU_DTYPE), MXU_DTYPE)
            table = b_rel_bias[jb]
            o = jnp.zeros((n, d), MXU_DTYPE)
            o = _band(q, kb[0], vb[0], _band_bias(table, tq_b, window, rel_clip), o, row0=0, rows=t_p,
                      tq=tq_b, k_pos0=-window, head_dim=head_dim, attn_scale=attn_scale)
            o = _band(q, kb[1], vb[1], _band_bias(table, t_s, window, rel_clip), o, row0=np_rows, rows=t_s,
                      tq=t_s, k_pos0=past - window, head_dim=head_dim, attn_scale=attn_scale)
            w_o = b_w_o[jb]
        x, xb = _mm_ln(o, w_o.astype(MXU_DTYPE), x, ln_mix_g[layer], ln_mix_b[layer], alpha)
        x, xb = _moe(x, moe_w_group[layer], moe_b_group[layer], moe_w_router[layer], moe_b_router[layer],
                     moe_w_gate, moe_w_up, moe_w_down, layer, ln_ffn_g[layer], ln_ffn_b[layer], alpha)
        if layer == n_a - 1:
            kb_new, kb_mx = _mm_dual(xb, b_w_kv[:, :d].astype(MXU_DTYPE))
            vb_new, vb_mx = _mm_dual(xb, b_w_kv[:, d:].astype(MXU_DTYPE))

            def band_ctx(arr, cache):
                ctx_p = jnp.pad(arr[:np_rows].reshape(nb_p, t_p, d), ((0, 0), (window, 0), (0, 0)))
                ctx_s = jnp.concatenate([cache.astype(MXU_DTYPE).reshape(nb_s, window, d),
                                         arr[np_rows:].reshape(nb_s, t_s, d)], axis=1)
                return ctx_p, ctx_s

            kb = band_ctx(kb_mx, cache_b_k)
            vb = band_ctx(vb_mx, cache_b_v)

    keep = min(window, t_p)

    def split(arrs, shape_tail):
        st = jnp.stack(arrs)
        return (st[:, :np_rows].reshape((n_a, nb_p, t_p) + shape_tail),
                st[:, np_rows:].reshape((n_a, nb_s, t_s) + shape_tail))

    a_k_p, a_k_s = split(new_k, (n_kv, head_dim))
    a_v_p, a_v_s = split(new_v, (n_kv, head_dim))
    a_i_p, a_i_s = split(new_i, (idx_dim,))
    heads_b = (n_heads, head_dim)
    b_k_p = kb_new[:np_rows].reshape((nb_p, t_p) + heads_b)[:, t_p - keep:]
    b_v_p = vb_new[:np_rows].reshape((nb_p, t_p) + heads_b)[:, t_p - keep:]
    b_k_s = kb_new[np_rows:].reshape((nb_s, t_s) + heads_b)
    b_v_s = vb_new[np_rows:].reshape((nb_s, t_s) + heads_b)
    return (x[:np_rows].reshape(nb_p, t_p, d), x[np_rows:].reshape(nb_s, t_s, d),
            a_k_p, a_v_p, a_i_p, b_k_p, b_v_p, a_k_s, a_v_s, a_i_s, b_k_s, b_v_s)
```

```python
import functools
import math

import jax
import jax.numpy as jnp
from jax import lax
from jax.experimental import pallas as pl
from jax.experimental.pallas import tpu as pltpu

CHUNK = 64
TOPK_MAX = 256
LEFT_CHUNKS = 8
LN_EPS = 1e-5

LANES = 128
VMEM_LIMIT_BYTES = 56 * 1024 * 1024
MXU_DTYPE = jnp.bfloat16

MM_ROWS = 768
LN_ROWS = 384
DMA_ROWS = 256
COMBINE_PARTS = 4
MOE_BLOCK_ROWS = 512
ROUTE_FIELDS = 8
DSA_TILE = 256
SEARCH_ROWS = 128
BAND_TILE = 256
BAND_HEADS = 8

F32 = jnp.float32
I32 = jnp.int32
NEG_INF = float("-inf")
INT_MIN = -(2 ** 31)
LOG2E = math.log2(math.e)


def _pick_tile(n, cap, mult):
    best = None
    for t in range(mult, min(n, cap) + 1, mult):
        if n % t == 0:
            best = t
    if best is None:
        raise ValueError(f"no tile for n={n} cap={cap} mult={mult}")
    return best


def _params(*sem):
    return pltpu.CompilerParams(dimension_semantics=sem, vmem_limit_bytes=VMEM_LIMIT_BYTES)


def _dot(a, b):
    return jnp.dot(a, b, preferred_element_type=F32)


def _dot_nt(a, b):
    return lax.dot_general(a, b, (((1,), (1,)), ((), ())), preferred_element_type=F32)


def _layer_norm(z, g, b):
    mu = jnp.mean(z, axis=-1, keepdims=True)
    zc = z - mu
    var = jnp.mean(zc * zc, axis=-1, keepdims=True)
    return zc * lax.rsqrt(var + LN_EPS) * g + b


def _mm_kernel(x_ref, w_ref, o_ref):
    o_ref[...] = _dot(x_ref[...].astype(MXU_DTYPE), w_ref[...]).astype(o_ref.dtype)


def _mm(x, w, out_dtype):
    n, k = x.shape
    m = w.shape[1]
    tm = _pick_tile(n, MM_ROWS, 16)
    tn = _pick_tile(m, 2048, LANES)
    return pl.pallas_call(
        _mm_kernel,
        grid=(m // tn, n // tm),
        in_specs=[pl.BlockSpec((tm, k), lambda j, i: (i, 0)),
                  pl.BlockSpec((k, tn), lambda j, i: (0, j))],
        out_specs=pl.BlockSpec((tm, tn), lambda j, i: (i, j)),
        out_shape=jax.ShapeDtypeStruct((n, m), out_dtype),
        name="projection",
        compiler_params=_params("arbitrary", "arbitrary"),
    )(x, w)


def _mm_dual_kernel(x_ref, w_ref, o_ref, ob_ref):
    y = _dot(x_ref[...].astype(MXU_DTYPE), w_ref[...])
    o_ref[...] = y
    ob_ref[...] = y.astype(ob_ref.dtype)


def _mm_dual(x, w):
    n, k = x.shape
    m = w.shape[1]
    tm = _pick_tile(n, MM_ROWS, 16)
    return pl.pallas_call(
        _mm_dual_kernel,
        grid=(n // tm,),
        in_specs=[pl.BlockSpec((tm, k), lambda i: (i, 0)),
                  pl.BlockSpec((k, m), lambda i: (0, 0))],
        out_specs=[pl.BlockSpec((tm, m), lambda i: (i, 0)), pl.BlockSpec((tm, m), lambda i: (i, 0))],
        out_shape=[jax.ShapeDtypeStruct((n, m), F32), jax.ShapeDtypeStruct((n, m), MXU_DTYPE)],
        name="projection_dual",
        compiler_params=_params("arbitrary"),
    )(x, w)


def _mm_ln_kernel(a_ref, w_ref, x_ref, g_ref, b_ref, o_ref, ob_ref, *, alpha):
    y = _dot(a_ref[...], w_ref[...])
    out = _layer_norm(alpha * x_ref[...] + y, g_ref[...], b_ref[...])
    o_ref[...] = out
    ob_ref[...] = out.astype(ob_ref.dtype)


def _mm_ln(a, w, x, g, b, alpha):
    n, k = a.shape
    d = w.shape[1]
    tm = _pick_tile(n, LN_ROWS, 16)
    row = lambda i: (i, 0)
    fixed = lambda i: (0, 0)
    return pl.pallas_call(
        functools.partial(_mm_ln_kernel, alpha=alpha),
        grid=(n // tm,),
        in_specs=[pl.BlockSpec((tm, k), row), pl.BlockSpec((k, d), fixed),
                  pl.BlockSpec((tm, d), row), pl.BlockSpec((1, d), fixed),
                  pl.BlockSpec((1, d), fixed)],
        out_specs=[pl.BlockSpec((tm, d), row), pl.BlockSpec((tm, d), row)],
        out_shape=[jax.ShapeDtypeStruct((n, d), F32), jax.ShapeDtypeStruct((n, d), MXU_DTYPE)],
        name="out_projection_norm",
        compiler_params=_params("arbitrary"),
    )(a, w, x, g.reshape(1, d), b.reshape(1, d))


def _row_total(acc, ones_blk):
    return _dot(acc.astype(MXU_DTYPE), ones_blk)


def _dsa_kernel(q_ref, qi_ref, wq_ref, k_ref, v_ref, kia_ref, kib_ref, prev_ref, o_ref,
                key_ref, msk_ref, *, tq, lk, l_valid, q_pos0, k_top, n_kv, group,
                n_idx_heads, idx_dim, head_dim, slopes, idx_scale, attn_scale):
    del prev_ref
    nslab = lk // LANES
    q_pos = q_pos0 + lax.broadcasted_iota(I32, (tq, lk), 0)
    k_pos = lax.broadcasted_iota(I32, (tq, lk), 1)
    admissible = ((k_pos // CHUNK) <= (q_pos // CHUNK)) & (k_pos < l_valid)

    wq = wq_ref[...]
    score = jnp.zeros((tq, lk), F32)
    for p in range(n_idx_heads // 2):
        qi2 = qi_ref[:, p * 2 * idx_dim:(p + 1) * 2 * idx_dim]
        s_a = _dot_nt(qi2, kia_ref[0])
        s_b = _dot_nt(qi2, kib_ref[0])
        w_a = wq[:, idx_dim + 2 * p:idx_dim + 2 * p + 1]
        w_b = wq[:, idx_dim + 2 * p + 1:idx_dim + 2 * p + 2]
        score = score + w_a * jnp.maximum(s_a, 0.0) + w_b * jnp.maximum(s_b, 0.0)
    score = score * idx_scale + 0.0

    bits = pltpu.bitcast(score, I32)
    key = bits ^ ((bits >> 31) & 0x7FFFFFFF)
    key_ref[...] = jnp.where(admissible, key, INT_MIN)

    ones_blk = jnp.ones((LANES, LANES), MXU_DTYPE)
    rows = min(tq, SEARCH_ROWS)
    groups = [slice(c * rows, (c + 1) * rows) for c in range(tq // rows)]

    def count(rs, pred):
        acc = jnp.zeros((rows, LANES), F32)
        for c in range(nslab):
            acc = acc + jnp.where(pred(key_ref[rs, c * LANES:(c + 1) * LANES]), 1.0, 0.0)
        return jnp.sum(acc, axis=-1, keepdims=True)

    def lanes(col):
        return jnp.broadcast_to(col, (rows, LANES))

    kf = float(k_top)
    thr0 = tuple(jnp.where(count(rs, lambda x: x >= 0) >= kf, 0, INT_MIN).astype(I32) for rs in groups)

    def bit_step(i, thrs):
        bit = lax.shift_left(jnp.int32(1), jnp.asarray(30 - i, dtype=I32))
        out = []
        for rs, t in zip(groups, thrs):
            cand = t | bit
            cand_l = lanes(cand)
            out.append(jnp.where(count(rs, lambda x, c=cand_l: x >= c) >= kf, cand, t))
        return tuple(out)

    thrs = lax.fori_loop(0, 31, bit_step, thr0)

    for rs, thr in zip(groups, thrs):
        thr = lanes(jnp.maximum(thr, INT_MIN + 1))
        n_gt = count(rs, lambda x, thr=thr: x > thr)
        n_eq = count(rs, lambda x, thr=thr: x == thr)
        for c in range(nslab):
            sl = slice(c * LANES, (c + 1) * LANES)
            msk_ref[rs, sl] = jnp.where(key_ref[rs, sl] >= thr, 0.0, NEG_INF)

        has_ties = jnp.max(jnp.where(n_gt + n_eq > kf, 1.0, 0.0)) > 0.0

        @pl.when(has_ties)
        def _(rs=rs, thr=thr, n_gt=n_gt):
            need = kf - n_gt
            upper = (lax.broadcasted_iota(I32, (LANES, LANES), 0)
                     < lax.broadcasted_iota(I32, (LANES, LANES), 1)).astype(MXU_DTYPE)
            before = jnp.zeros((rows, LANES), F32)
            for c in range(nslab):
                sl = slice(c * LANES, (c + 1) * LANES)
                ks = key_ref[rs, sl]
                eq = jnp.where(ks == thr, 1.0, 0.0)
                rank = before + _dot(eq.astype(MXU_DTYPE), upper)
                keep = (ks > thr) | ((ks == thr) & (rank < need))
                msk_ref[rs, sl] = jnp.where(keep, 0.0, NEG_INF)
                before = before + _row_total(eq, ones_blk)

    dist = jnp.abs(q_pos - k_pos).astype(F32)
    for g in range(n_kv):
        cols = slice(g * head_dim, (g + 1) * head_dim)
        kg = k_ref[0, :, cols]
        vg = v_ref[0, :, cols]
        q4 = jnp.concatenate(
            [q_ref[:, (g * group + r) * head_dim:(g * group + r + 1) * head_dim] for r in range(group)],
            axis=0)
        logits = _dot_nt(q4, kg)
        for r in range(group):
            h = g * group + r
            z = logits[r * tq:(r + 1) * tq] * (attn_scale * LOG2E) - (slopes[h] * LOG2E) * dist + msk_ref[...]
            m = jnp.max(z, axis=-1, keepdims=True)
            p = jnp.exp2(z - m)
            denom = jnp.sum(p, axis=-1, keepdims=True)
            o = _dot(p.astype(MXU_DTYPE), vg) / denom
            o_ref[:, h * head_dim:(h + 1) * head_dim] = o.astype(o_ref.dtype)


def _dsa(q, qi, kiwi, k, v, kia, kib, prev, *, row0, rows, tq, tile, lk, l_valid, q_pos0, k_top, cfg):
    nb, _, a_kv = k.shape
    n_rows, a_q = q.shape
    n_q = rows // tq
    blk0 = row0 // tq
    qmap = lambda b: (blk0 + b * n_q + tile, 0)
    kmap = lambda b: (b, 0, 0)
    kern = functools.partial(
        _dsa_kernel, tq=tq, lk=lk, l_valid=l_valid, q_pos0=q_pos0, k_top=k_top,
        n_kv=cfg["n_kv"], group=cfg["group"], n_idx_heads=cfg["n_idx_heads"],
        idx_dim=cfg["idx_dim"], head_dim=cfg["head_dim"], slopes=cfg["slopes"],
        idx_scale=cfg["idx_scale"], attn_scale=cfg["attn_scale"])
    return pl.pallas_call(
        kern,
        grid=(nb,),
        in_specs=[pl.BlockSpec((tq, a_q), qmap),
                  pl.BlockSpec((tq, qi.shape[1]), qmap),
                  pl.BlockSpec((tq, LANES), qmap),
                  pl.BlockSpec((1, lk, a_kv), kmap),
                  pl.BlockSpec((1, lk, a_kv), kmap),
                  pl.BlockSpec((1, lk, LANES), kmap),
                  pl.BlockSpec((1, lk, LANES), kmap),
                  pl.BlockSpec(memory_space=pl.ANY)],
        out_specs=pl.BlockSpec((tq, a_q), qmap),
        out_shape=jax.ShapeDtypeStruct((n_rows, a_q), MXU_DTYPE),
        scratch_shapes=[pltpu.VMEM((tq, lk), I32), pltpu.VMEM((tq, lk), F32)],
        input_output_aliases={7: 0},
        name="dsa_attention",
        compiler_params=_params("arbitrary"),
    )(q, qi, kiwi, k, v, kia, kib, prev)


def _band_kernel(q_ref, k_ref, v_ref, bias_ref, prev_ref, o_ref, *, tq, win, k_pos0, heads, head_dim,
                 attn_scale):
    del prev_ref
    j = pl.program_id(2)
    start = pl.multiple_of(j * tq, tq)
    k_pos = k_pos0 + j * tq + lax.broadcasted_iota(I32, (tq, win), 1)
    before_start = jnp.where(k_pos >= 0, 0.0, NEG_INF)
    for r in range(heads):
        cols = slice(r * head_dim, (r + 1) * head_dim)
        kw = k_ref[0, r, pl.ds(start, win), :]
        vw = v_ref[0, r, pl.ds(start, win), :]
        z = _dot_nt(q_ref[:, cols], kw) * (attn_scale * LOG2E) + bias_ref[r] + before_start
        m = jnp.max(z, axis=-1, keepdims=True)
        p = jnp.exp2(z - m)
        denom = jnp.sum(p, axis=-1, keepdims=True)
        o = _dot(p.astype(MXU_DTYPE), vw) / denom
        o_ref[:, cols] = o.astype(o_ref.dtype)


def _band(q, kpad, vpad, bias, prev, *, row0, rows, tq, k_pos0, head_dim, attn_scale):
    nb, _, lk, _ = kpad.shape
    n_rows, d = q.shape
    n_heads, _, win = bias.shape
    hg = BAND_HEADS if n_heads % BAND_HEADS == 0 else 1
    wcol = hg * head_dim
    n_q = rows // tq
    blk0 = row0 // tq
    qmap = lambda b, g, j: (blk0 + b * n_q + j, g)
    kmap = lambda b, g, j: (b, g, 0, 0)
    kern = functools.partial(_band_kernel, tq=tq, win=win, k_pos0=k_pos0, heads=hg, head_dim=head_dim,
                             attn_scale=attn_scale)
    return pl.pallas_call(
        kern,
        grid=(nb, n_heads // hg, n_q),
        in_specs=[pl.BlockSpec((tq, wcol), qmap),
                  pl.BlockSpec((1, hg, lk, head_dim), kmap),
                  pl.BlockSpec((1, hg, lk, head_dim), kmap),
                  pl.BlockSpec((hg, tq, win), lambda b, g, j: (g, 0, 0)),
                  pl.BlockSpec(memory_space=pl.ANY)],
        out_specs=pl.BlockSpec((tq, wcol), qmap),
        out_shape=jax.ShapeDtypeStruct((n_rows, d), MXU_DTYPE),
        input_output_aliases={4: 0},
        name="band_attention",
        compiler_params=_params("arbitrary", "arbitrary", "arbitrary"),
    )(q, kpad, vpad, bias, prev)


def _band_bias(table, tq, window, rel_clip):
    n_heads = table.shape[0]
    w = window + tq
    ext = w + tq - 1
    x = jnp.arange(ext, dtype=I32)
    rel = jnp.clip(window + tq - 1 - x, -rel_clip, rel_clip) + rel_clip
    e = jnp.pad(table[:, rel].astype(F32) * LOG2E, ((0, 0), (0, 1)))
    skew = jnp.broadcast_to(e[:, None, :], (n_heads, tq, ext + 1)).reshape(n_heads, tq * (ext + 1))
    skew = skew[:, :tq * ext].reshape(n_heads, tq, ext)[:, :, tq - 1:tq - 1 + w]
    qi = jnp.arange(tq, dtype=I32)[:, None]
    kj = jnp.arange(w, dtype=I32)[None, :]
    qc = qi // CHUNK
    kc = kj // CHUNK - window // CHUNK
    band = (kc <= qc) & (kc >= qc - LEFT_CHUNKS)
    return jnp.where(band[None], skew, NEG_INF)


def _router_kernel(x_ref, wh_ref, wl_ref, b_ref, rt_ref, rw_ref, cnt_ref, run_ref, *, tm, n_groups,
                   per_group):
    i = pl.program_id(0)

    @pl.when(i == 0)
    def _():
        run_ref[...] = jnp.zeros_like(run_ref)

    x = x_ref[...]
    xh = x.astype(MXU_DTYPE)
    xl = (x - xh.astype(F32)).astype(MXU_DTYPE)
    logits = _dot(xh, wh_ref[...]) + _dot(xl, wh_ref[...]) + _dot(xh, wl_ref[...]) + b_ref[...]

    lane = lax.broadcasted_iota(I32, (tm, LANES), 1)
    gl = jnp.where(lane < n_groups, logits, NEG_INF)
    gmax = jnp.max(gl, axis=-1, keepdims=True)
    gsel = jnp.min(jnp.where(gl == gmax, lane, LANES), axis=-1, keepdims=True)
    p_group = 1.0 / jnp.sum(jnp.exp(gl - gmax), axis=-1, keepdims=True)

    lo = n_groups + gsel * per_group
    el = jnp.where((lane >= lo) & (lane < lo + per_group), logits, NEG_INF)
    v1 = jnp.max(el, axis=-1, keepdims=True)
    i1 = jnp.min(jnp.where(el == v1, lane, LANES), axis=-1, keepdims=True)
    el2 = jnp.where(lane == i1, NEG_INF, el)
    v2 = jnp.max(el2, axis=-1, keepdims=True)
    i2 = jnp.min(jnp.where(el2 == v2, lane, LANES), axis=-1, keepdims=True)
    t = jnp.exp(v2 - v1)
    w1 = p_group / (1.0 + t)
    w2 = p_group * t / (1.0 + t)
    e1 = i1 - n_groups
    e2 = i2 - n_groups

    oh1 = jnp.where(lane == e1, 1.0, 0.0)
    oh2 = jnp.where(lane == e2, 1.0, 0.0)
    lower = (lax.broadcasted_iota(I32, (tm, tm), 1)
             < lax.broadcasted_iota(I32, (tm, tm), 0)).astype(MXU_DTYPE)
    pre1 = _dot(lower, oh1.astype(MXU_DTYPE))
    pre2 = _dot(lower, oh2.astype(MXU_DTYPE))
    c1 = jnp.sum(oh1, axis=0, keepdims=True)
    c2 = jnp.sum(oh2, axis=0, keepdims=True)
    run = run_ref[...]
    r1 = jnp.sum(oh1 * (pre1 + run), axis=-1, keepdims=True)
    r2 = jnp.sum(oh2 * (pre2 + run + c1), axis=-1, keepdims=True)
    run = run + c1 + c2
    run_ref[...] = run
    cnt_ref[...] = run

    fields = jnp.where(lane == 0, e1.astype(F32), jnp.where(lane == 1, e2.astype(F32), jnp.where(
        lane == 2, r1, jnp.where(lane == 3, r2, 0.0))))
    rt_ref[...] = fields.T[0:ROUTE_FIELDS]
    rw_ref[...] = jnp.where(lane == 0, w1, jnp.where(lane == 1, w2, 0.0))


def _router(x, wh, wl, bias, n_groups, per_group):
    n, d = x.shape
    tm = _pick_tile(n, LN_ROWS, LANES)
    row = lambda i: (i, 0)
    fixed = lambda i: (0, 0)
    return pl.pallas_call(
        functools.partial(_router_kernel, tm=tm, n_groups=n_groups, per_group=per_group),
        grid=(n // tm,),
        in_specs=[pl.BlockSpec((tm, d), row), pl.BlockSpec((d, LANES), fixed),
                  pl.BlockSpec((d, LANES), fixed), pl.BlockSpec((1, LANES), fixed)],
        out_specs=[pl.BlockSpec((ROUTE_FIELDS, tm), lambda i: (0, i)), pl.BlockSpec((tm, LANES), row),
                   pl.BlockSpec((1, LANES), fixed)],
        out_shape=[jax.ShapeDtypeStruct((ROUTE_FIELDS, n), F32), jax.ShapeDtypeStruct((n, LANES), F32),
                   jax.ShapeDtypeStruct((1, LANES), F32)],
        scratch_shapes=[pltpu.VMEM((1, LANES), F32)],
        name="moe_router",
        compiler_params=_params("arbitrary"),
    )(x, wh, wl, bias)


def _dispatch_kernel(pend_ref, d0_ref, d1_ref, x_ref, xs_ref, zero_ref, sem, zsem, *, tm, tb, n_experts):
    @pl.when(pl.program_id(0) == 0)
    def _():
        zero_ref[...] = jnp.zeros_like(zero_ref)

        def block_copy(e):
            first = pl.multiple_of(jnp.maximum(pend_ref[e] - tb, 0), tb)
            return pltpu.make_async_copy(zero_ref, xs_ref.at[pl.ds(first, tb)], zsem)

        def nonempty(e):
            return pend_ref[e] > (pend_ref[e - 1] if e else 0)

        for e in range(n_experts):
            pl.when(nonempty(e))(lambda e=e: block_copy(e).start())
        for e in range(n_experts):
            pl.when(nonempty(e))(lambda e=e: block_copy(e).wait())

        def tail_copy(j):
            return pltpu.make_async_copy(zero_ref, xs_ref.at[pl.ds(pl.multiple_of(j * tb, tb), tb)], zsem)

        n_used = pend_ref[n_experts - 1] // tb
        n_blocks = xs_ref.shape[0] // tb
        lax.fori_loop(n_used, n_blocks, lambda j, c: (tail_copy(j).start(), c)[1], 0)
        lax.fori_loop(n_used, n_blocks, lambda j, c: (tail_copy(j).wait(), c)[1], 0)

    dest_refs = (d0_ref, d1_ref)

    def row_copy(r, s):
        return pltpu.make_async_copy(x_ref.at[pl.ds(r, 1)], xs_ref.at[pl.ds(dest_refs[s][0, 0, r], 1)], sem)

    for r in range(tm):
        row_copy(r, 0).start(priority=0)
        row_copy(r, 1).start(priority=1)
    for r in range(tm):
        row_copy(r, 0).wait()
        row_copy(r, 1).wait()


def _dispatch(x, dests, pend, p_rows, tb):
    n, d = x.shape
    tm = dests[0].shape[2]
    slot_spec = pl.BlockSpec((1, 1, tm), lambda i, pe: (i, 0, 0), memory_space=pltpu.SMEM)
    grid_spec = pltpu.PrefetchScalarGridSpec(
        num_scalar_prefetch=1,
        grid=(n // tm,),
        in_specs=[slot_spec, slot_spec, pl.BlockSpec((tm, d), lambda i, pe: (i, 0))],
        out_specs=pl.BlockSpec(memory_space=pl.ANY),
        scratch_shapes=[pltpu.VMEM((tb, d), x.dtype), pltpu.SemaphoreType.DMA, pltpu.SemaphoreType.DMA])
    return pl.pallas_call(
        functools.partial(_dispatch_kernel, tm=tm, tb=tb, n_experts=pend.shape[0]),
        grid_spec=grid_spec,
        out_shape=jax.ShapeDtypeStruct((p_rows, d), x.dtype),
        name="moe_dispatch",
        compiler_params=_params("arbitrary"),
    )(pend, dests[0], dests[1], x)


def _expert_kernel(be_ref, nu_ref, xs_ref, wg_ref, wu_ref, wd_ref, ys_ref, wg_s, wu_s, wd_s):
    i = pl.program_id(0)
    used = i < nu_ref[0]
    new_expert = jnp.logical_or(i == 0, be_ref[i] != be_ref[jnp.maximum(i - 1, 0)])

    @pl.when(jnp.logical_and(used, new_expert))
    def _():
        wg_s[...] = wg_ref[0, 0].astype(MXU_DTYPE)
        wu_s[...] = wu_ref[0, 0].astype(MXU_DTYPE)
        wd_s[...] = wd_ref[0, 0].astype(MXU_DTYPE)

    @pl.when(used)
    def _():
        x = xs_ref[...].astype(MXU_DTYPE)
        g = _dot(x, wg_s[...])
        u = _dot(x, wu_s[...])
        h = g * (1.0 / (1.0 + jnp.exp(-g))) * u
        ys_ref[...] = _dot(h.astype(MXU_DTYPE), wd_s[...])

    @pl.when(jnp.logical_not(used))
    def _():
        ys_ref[...] = jnp.zeros_like(ys_ref)


def _experts(xs, block_e, n_used, wg, wu, wd, layer, tb):
    p_rows, d = xs.shape
    f = wg.shape[3]
    nb = p_rows // tb
    rmap = lambda i, be, nu: (jnp.minimum(i, nu[0] - 1), 0)
    wmap = lambda i, be, nu: (layer, be[i], 0, 0)
    grid_spec = pltpu.PrefetchScalarGridSpec(
        num_scalar_prefetch=2,
        grid=(nb,),
        in_specs=[pl.BlockSpec((tb, d), rmap), pl.BlockSpec((1, 1, d, f), wmap),
                  pl.BlockSpec((1, 1, d, f), wmap), pl.BlockSpec((1, 1, f, d), wmap)],
        out_specs=pl.BlockSpec((tb, d), lambda i, be, nu: (i, 0)),
        scratch_shapes=[pltpu.VMEM((d, f), MXU_DTYPE), pltpu.VMEM((d, f), MXU_DTYPE),
                        pltpu.VMEM((f, d), MXU_DTYPE)])
    return pl.pallas_call(
        _expert_kernel,
        grid_spec=grid_spec,
        out_shape=jax.ShapeDtypeStruct((p_rows, d), F32),
        name="moe_experts",
        compiler_params=_params("arbitrary"),
    )(block_e, n_used, xs, wg, wu, wd)


def _combine_kernel(d0_ref, d1_ref, n0_ref, n1_ref, x_ref, rw_ref, g_ref, b_ref, ys_ref, o_ref, ob_ref, buf,
                    sem, *, tm, alpha):
    i = pl.program_id(0)
    last = pl.num_programs(0) - 1
    slot = i % 2
    other = 1 - slot
    here = (d0_ref, d1_ref)
    ahead = (n0_ref, n1_ref)

    def row_copy(drefs, r, s, sl):
        return pltpu.make_async_copy(ys_ref.at[pl.ds(drefs[s][0, 0, r], 1)], buf.at[sl, s, pl.ds(r, 1)],
                                     sem.at[sl])

    def issue(drefs, sl, rows):
        for r in rows:
            row_copy(drefs, r, 0, sl).start(priority=0)
            row_copy(drefs, r, 1, sl).start(priority=1)

    def wait_all(sl):
        for r in range(tm):
            row_copy(here, r, 0, sl).wait()
            row_copy(here, r, 1, sl).wait()

    @pl.when(i == 0)
    def _():
        issue(here, 0, range(tm))

    wait_all(slot)

    rows_per_part = tm // COMBINE_PARTS
    for part in range(COMBINE_PARTS):
        rs = slice(part * rows_per_part, (part + 1) * rows_per_part)
        issue(ahead, other, range(rs.start, rs.stop))
        rw = rw_ref[rs]
        m = rw[:, 0:1] * buf[slot, 0, rs] + rw[:, 1:2] * buf[slot, 1, rs]
        out = _layer_norm(alpha * x_ref[rs] + m, g_ref[...], b_ref[...])
        o_ref[rs] = out
        ob_ref[rs] = out.astype(ob_ref.dtype)

    @pl.when(i == last)
    def _():
        wait_all(other)


def _combine(x, ys, dests, rw, g, b, alpha):
    n, d = x.shape
    n_steps, _, tm = dests[0].shape
    row = lambda i: (i, 0)
    fixed = lambda i: (0, 0)
    here = pl.BlockSpec((1, 1, tm), lambda i: (i, 0, 0), memory_space=pltpu.SMEM)
    ahead = pl.BlockSpec((1, 1, tm), lambda i: (jnp.minimum(i + 1, n_steps - 1), 0, 0), memory_space=pltpu.SMEM)
    return pl.pallas_call(
        functools.partial(_combine_kernel, tm=tm, alpha=alpha),
        grid=(n_steps,),
        in_specs=[here, here, ahead, ahead,
                  pl.BlockSpec((tm, d), row), pl.BlockSpec((tm, LANES), row),
                  pl.BlockSpec((1, d), fixed), pl.BlockSpec((1, d), fixed),
                  pl.BlockSpec(memory_space=pl.ANY)],
        out_specs=[pl.BlockSpec((tm, d), row), pl.BlockSpec((tm, d), row)],
        out_shape=[jax.ShapeDtypeStruct((n, d), F32), jax.ShapeDtypeStruct((n, d), MXU_DTYPE)],
        scratch_shapes=[pltpu.VMEM((2, 2, tm, d), F32), pltpu.SemaphoreType.DMA((2,))],
        name="moe_combine_norm",
        compiler_params=_params("arbitrary"),
    )(dests[0], dests[1], dests[0], dests[1], x, rw, g.reshape(1, d), b.reshape(1, d), ys)


def _moe(x, w_group, b_group, w_router, b_router, wg, wu, wd, layer, ln_g, ln_b, alpha):
    n, d = x.shape
    tb = MOE_BLOCK_ROWS
    n_groups = w_group.shape[1]
    n_experts = w_router.shape[1]
    per_group = n_experts // n_groups
    pad = LANES - n_groups - n_experts
    w_all = jnp.pad(jnp.concatenate([w_group, w_router], axis=1), ((0, 0), (0, pad)))
    b_all = jnp.pad(jnp.concatenate([b_group, b_router]), (0, pad)).reshape(1, LANES)
    wh = w_all.astype(MXU_DTYPE)
    wl = (w_all - wh.astype(F32)).astype(MXU_DTYPE)
    rt, rw, cnt = _router(x, wh, wl, b_all, n_groups, per_group)

    counts = cnt[0, :n_experts].astype(I32)
    padded = (counts + tb - 1) // tb * tb
    pend = jnp.cumsum(padded)
    poff = pend - padded
    expert = rt[0:2].astype(I32)
    offset = jnp.sum(jnp.where(expert[..., None] == jnp.arange(n_experts, dtype=I32), poff, 0), axis=-1)
    dest = offset + rt[2:4].astype(I32)
    tm = _pick_tile(n, DMA_ROWS, 16 * COMBINE_PARTS)
    dests = (dest[0].reshape(n // tm, 1, tm), dest[1].reshape(n // tm, 1, tm))
    p_rows = (2 * n + n_experts * (tb - 1) + tb - 1) // tb * tb
    starts = jnp.arange(p_rows // tb, dtype=I32) * tb
    block_e = jnp.minimum(jnp.sum(pend[None, :] <= starts[:, None], axis=1), n_experts - 1).astype(I32)
    n_used = (pend[-1:] // tb).astype(I32)

    xs = _dispatch(x, dests, pend.astype(I32), p_rows, tb)
    ys = _experts(xs, block_e, n_used, wg, wu, wd, layer, tb)
    return _combine(x, ys, dests, rw, ln_g, ln_b, alpha)


def kernel(x_prompt, x_sample, cache_a_k, cache_a_v, cache_a_idx, cache_b_k, cache_b_v, a_w_in, a_w_o, b_w_q, b_w_kv, b_w_o, b_rel_bias, ln_mix_g, ln_mix_b, ln_ffn_g, ln_ffn_b, moe_w_group, moe_b_group, moe_w_router, moe_b_router, moe_w_gate, moe_w_up, moe_w_down):
    nb_p, t_p, d = x_prompt.shape
    nb_s, t_s, _ = x_sample.shape
    n_a, _, past, n_kv, head_dim = cache_a_k.shape
    idx_dim = cache_a_idx.shape[-1]
    depth = ln_mix_g.shape[0]
    n_heads = d // head_dim
    a_q = n_heads * head_dim
    a_kv = n_kv * head_dim
    n_idx_heads = (a_w_in.shape[-1] - a_q - 2 * a_kv - idx_dim) // (idx_dim + 1)
    a_qi = n_idx_heads * idx_dim
    window = LEFT_CHUNKS * CHUNK
    rel_clip = (b_rel_bias.shape[-1] - 1) // 2
    alpha = (2 * depth) ** 0.25
    attn_scale = head_dim ** -0.5
    np_rows = nb_p * t_p
    ns_rows = nb_s * t_s
    n = np_rows + ns_rows
    assert 2 * idx_dim == LANES and n_idx_heads % 2 == 0 and n_idx_heads <= LANES - idx_dim
    assert t_p % CHUNK == 0 and past % CHUNK == 0 and t_s <= CHUNK and window % t_s == 0
    assert cache_b_k.shape[1] == window

    cfg = dict(n_kv=n_kv, group=n_heads // n_kv, n_idx_heads=n_idx_heads, idx_dim=idx_dim,
               head_dim=head_dim, attn_scale=attn_scale,
               idx_scale=idx_dim ** -0.5 * n_idx_heads ** -0.5,
               slopes=tuple(2.0 ** (-8.0 * (h + 1) / n_heads) for h in range(n_heads)))

    tq_a = _pick_tile(t_p, DSA_TILE, LANES)
    tq_b = _pick_tile(t_p, BAND_TILE, CHUNK)
    l_s = past + t_s
    lp_s = -(-l_s // LANES) * LANES

    x = jnp.concatenate([x_prompt.reshape(np_rows, d), x_sample.reshape(ns_rows, d)], axis=0)
    xb = x
    new_k, new_v, new_i = [], [], []
    kb = vb = None
    for layer in range(depth):
        if layer < n_a:
            w = a_w_in[layer]
            o3 = a_q + 2 * a_kv
            w_kiwi = jnp.pad(w[:, o3 + a_qi:], ((0, 0), (0, LANES - idx_dim - n_idx_heads)))
            q = _mm(xb, w[:, :a_q].astype(MXU_DTYPE), MXU_DTYPE)
            k_new, k_mx = _mm_dual(xb, w[:, a_q:a_q + a_kv].astype(MXU_DTYPE))
            v_new, v_mx = _mm_dual(xb, w[:, a_q + a_kv:o3].astype(MXU_DTYPE))
            qi = _mm(xb, w[:, o3:o3 + a_qi].astype(MXU_DTYPE), MXU_DTYPE)
            kiwi = _mm(xb, w_kiwi.astype(MXU_DTYPE), F32)
            ki_new = kiwi[:, :idx_dim]
            ki_mx = ki_new.astype(MXU_DTYPE)
            new_k.append(k_new)
            new_v.append(v_new)
            new_i.append(ki_new)

            def key_side(arr, cache, width):
                kp = arr[:np_rows].reshape(nb_p, t_p, width)
                ks = jnp.concatenate([cache.astype(MXU_DTYPE).reshape(nb_s, past, width),
                                      arr[np_rows:].reshape(nb_s, t_s, width)], axis=1)
                return kp, jnp.pad(ks, ((0, 0), (0, lp_s - l_s), (0, 0)))

            k_p, k_s = key_side(k_mx, cache_a_k[layer], a_kv)
            v_p, v_s = key_side(v_mx, cache_a_v[layer], a_kv)
            i_p, i_s = key_side(ki_mx, cache_a_idx[layer], idx_dim)
            zpad = ((0, 0), (0, 0), (0, LANES - idx_dim))
            zpad_front = ((0, 0), (0, 0), (LANES - idx_dim, 0))
            keys_p = (k_p, v_p, jnp.pad(i_p, zpad), jnp.pad(i_p, zpad_front))
            keys_s = (k_s, v_s, jnp.pad(i_s, zpad), jnp.pad(i_s, zpad_front))
            o = jnp.zeros((n, a_q), MXU_DTYPE)
            for u in range(t_p // tq_a):
                o = _dsa(q, qi, kiwi, *keys_p, o, row0=0, rows=t_p, tq=tq_a, tile=u, lk=(u + 1) * tq_a,
                         l_valid=t_p, q_pos0=u * tq_a, k_top=min(TOPK_MAX, t_p // 4), cfg=cfg)
            o = _dsa(q, qi, kiwi, *keys_s, o, row0=np_rows, rows=t_s, tq=t_s, tile=0, lk=lp_s,
                     l_valid=l_s, q_pos0=past, k_top=min(TOPK_MAX, l_s // 4), cfg=cfg)
            w_o = a_w_o[layer]
        else:
            jb = layer - n_a
            q = _mm(xb, b_w_q[jb].astype(MXU_DTYPE), MXU_DTYPE)
            table = b_rel_bias[jb]
            o = jnp.zeros((n, d), MXU_DTYPE)
            o = _band(q, kb[0], vb[0], _band_bias(table, tq_b, window, rel_clip), o, row0=0, rows=t_p,
                      tq=tq_b, k_pos0=-window, head_dim=head_dim, attn_scale=attn_scale)
            o = _band(q, kb[1], vb[1], _band_bias(table, t_s, window, rel_clip), o, row0=np_rows, rows=t_s,
                      tq=t_s, k_pos0=past - window, head_dim=head_dim, attn_scale=attn_scale)
            w_o = b_w_o[jb]
        x, xb = _mm_ln(o, w_o.astype(MXU_DTYPE), x, ln_mix_g[layer], ln_mix_b[layer], alpha)
        x, xb = _moe(x, moe_w_group[layer], moe_b_group[layer], moe_w_router[layer], moe_b_router[layer],
                     moe_w_gate, moe_w_up, moe_w_down, layer, ln_ffn_g[layer], ln_ffn_b[layer], alpha)
        if layer == n_a - 1:
            kb_new, kb_mx = _mm_dual(xb, b_w_kv[:, :d].astype(MXU_DTYPE))
            vb_new, vb_mx = _mm_dual(xb, b_w_kv[:, d:].astype(MXU_DTYPE))

            def band_ctx(arr, cache):
                heads = (n_heads, head_dim)
                ctx_p = jnp.pad(arr[:np_rows].reshape((nb_p, t_p) + heads).transpose(0, 2, 1, 3),
                                ((0, 0), (0, 0), (window, 0), (0, 0)))
                ctx_s = jnp.concatenate([cache.astype(MXU_DTYPE), arr[np_rows:].reshape((nb_s, t_s) + heads)],
                                        axis=1).transpose(0, 2, 1, 3)
                return ctx_p, ctx_s

            kb = band_ctx(kb_mx, cache_b_k)
            vb = band_ctx(vb_mx, cache_b_v)

    keep = min(window, t_p)

    def split(arrs, shape_tail):
        st = jnp.stack(arrs)
        return (st[:, :np_rows].reshape((n_a, nb_p, t_p) + shape_tail),
                st[:, np_rows:].reshape((n_a, nb_s, t_s) + shape_tail))

    a_k_p, a_k_s = split(new_k, (n_kv, head_dim))
    a_v_p, a_v_s = split(new_v, (n_kv, head_dim))
    a_i_p, a_i_s = split(new_i, (idx_dim,))
    heads_b = (n_heads, head_dim)
    b_k_p = kb_new[:np_rows].reshape((nb_p, t_p) + heads_b)[:, t_p - keep:]
    b_v_p = vb_new[:np_rows].reshape((nb_p, t_p) + heads_b)[:, t_p - keep:]
    b_k_s = kb_new[np_rows:].reshape((nb_s, t_s) + heads_b)
    b_v_s = vb_new[np_rows:].reshape((nb_s, t_s) + heads_b)
    return (x[:np_rows].reshape(nb_p, t_p, d), x[np_rows:].reshape(nb_s, t_s, d),
            a_k_p, a_v_p, a_i_p, b_k_p, b_v_p, a_k_s, a_v_s, a_i_s, b_k_s, b_v_s)
```
